```python
import jax, jax.numpy as jnp
from jax import lax
import numpy as np

D_MODEL = 1024
BATCH = 4
SEQ = 4096
DEPTH = 2

GRID_W = 64
CTX_LEN = 256
N_EVEN = (DEPTH + 1) // 2
N_ODD = DEPTH // 2
H_A = 4
DK_A = 128
DV_A = 128
H_B = 4
DK_B = 64
DV_B = 128
CONV_W = 3
CHUNK = 32
MIX_W = H_A * DV_A + H_B * DV_B
N_HEADS_C = 8
N_KV_C = 2
GROUP_C = N_HEADS_C // N_KV_C
HEAD_DIM = 128
AXIS_DIM = HEAD_DIM // 2
Q_BLOCK = 128
ROPE_THETA = 10000.0
D_QKV_C = (N_HEADS_C + 2 * N_KV_C) * HEAD_DIM
D_FF = ((8 * D_MODEL // 3 + 255) // 256) * 256
AB_SPLIT = (H_A * DK_A, H_A * DV_A, H_A * DV_A, H_A * DK_A, H_A * DK_A,
            2 * H_B * DK_B, H_B * DV_B, H_B * DV_B, 4 * H_B)
D_AB_IN = sum(AB_SPLIT)

kernel_name = "hybrid_hgrn2_mlstm_gqa_prefix_dit"


def rmsnorm(x, g, eps=1e-6):
    xf = x.astype(jnp.float32)
    y = xf * lax.rsqrt(jnp.mean(xf * xf, axis=-1, keepdims=True) + eps)
    return (y * g.astype(jnp.float32)).astype(x.dtype)


def modulate(x, g, shift, scale):
    return rmsnorm(x, g) * (1 + scale) + shift


def ada_params(cvec, w, b):
    return jnp.split(jax.nn.silu(cvec) @ w + b, 6, axis=-1)


def swiglu(h, w_in, w_out):
    g, u = jnp.split(h @ w_in, 2, axis=-1)
    return (jax.nn.silu(g) * u) @ w_out


def conv_centred(x, w):
    return lax.conv_general_dilated(x, w[:, None, :], window_strides=(1,),
                                    padding=[(CONV_W // 2, CONV_W // 2)],
                                    dimension_numbers=('NWC', 'WIO', 'NWC'),
                                    feature_group_count=x.shape[-1])


def to_chunks(a):
    Bn, T = a.shape[:2]
    return jnp.moveaxis(a.reshape(Bn, T // CHUNK, CHUNK, *a.shape[2:]), 1, 0)


def from_chunks(a):
    nc, Bn, C = a.shape[:3]
    return jnp.moveaxis(a, 0, 1).reshape(Bn, nc * C, *a.shape[3:])


def hgrn2_chunk_scan(q, k, v, logf, state0):
    f32 = jnp.float32
    mask = jnp.tril(jnp.ones((CHUNK, CHUNK), bool))

    def step(S, xs):
        qc, kc, vc, fc = xs
        b = jnp.cumsum(fc, axis=1)
        b_mid = b[:, CHUNK // 2 - 1:CHUNK // 2]
        b_last = b[:, -1:]
        A = jnp.einsum('bthd,bshd->bhts', qc * jnp.exp(b - b_mid), kc * jnp.exp(b_mid - b))
        A = jnp.where(mask, A, 0.0)
        o = (jnp.einsum('bhts,bshe->bthe', A, vc)
             + jnp.einsum('bthd,bhde->bthe', qc * jnp.exp(b), S))
        S_new = (S * jnp.exp(b_last[:, 0])[..., None]
                 + jnp.einsum('bshd,bshe->bhde', kc * jnp.exp(b_last - b), vc))
        return S_new, o

    xs = tuple(to_chunks(a.astype(f32)) for a in (q, k, v, logf))
    S_fin, o = lax.scan(step, state0, xs)
    return from_chunks(o).astype(v.dtype), S_fin


def mlstm_chunk_scan(q, k, v, ig, logf, state0):
    f32 = jnp.float32
    mask = jnp.tril(jnp.ones((CHUNK, CHUNK), bool))

    def step(carry, xs):
        C, n, m = carry
        qc, kc, vc, igc, fc = xs
        bh = jnp.swapaxes(jnp.cumsum(fc, axis=1), 1, 2)
        igh = jnp.swapaxes(igc, 1, 2)
        D = jnp.where(mask, bh[..., :, None] - bh[..., None, :] + igh[..., None, :], -jnp.inf)
        m_inter = bh + m[..., None]
        m_t = jnp.maximum(D.max(-1), m_inter)
        S = jnp.einsum('bthd,bshd->bhts', qc, kc) * jnp.exp(D - m_t[..., None])
        w_inter = jnp.exp(m_inter - m_t)
        num = (jnp.einsum('bhts,bshe->bthe', S, vc)
               + jnp.swapaxes(w_inter, 1, 2)[..., None] * jnp.einsum('bthd,bhde->bthe', qc, C))
        den = S.sum(-1) + w_inter * jnp.einsum('bthd,bhd->bht', qc, n)
        h = num / jnp.swapaxes(jnp.maximum(jnp.abs(den), jnp.exp(-m_t)), 1, 2)[..., None]
        b_last = bh[..., -1]
        g = b_last[..., None] - bh + igh
        m_new = jnp.maximum(b_last + m, g.max(-1))
        wk = jnp.exp(g - m_new[..., None])
        decay = jnp.exp(b_last + m - m_new)
        C_new = decay[..., None, None] * C + jnp.einsum('bhs,bshd,bshe->bhde', wk, kc, vc)
        n_new = decay[..., None] * n + jnp.einsum('bhs,bshd->bhd', wk, kc)
        return (C_new, n_new, m_new), h

    xs = tuple(to_chunks(a.astype(f32)) for a in (q, k, v, ig, logf))
    state_fin, h = lax.scan(step, state0, xs)
    return from_chunks(h).astype(v.dtype), state_fin


def bidirectional_prefix(scan_fn, ctx_dirs, lat_dirs, state0):
    outs_c, outs_l = [], []
    for d, reverse in enumerate((False, True)):
        flip = (lambda a: jnp.flip(a, axis=1)) if reverse else (lambda a: a)
        o_c, st = scan_fn(*[flip(a) for a in ctx_dirs[d]], state0)
        o_l, _ = scan_fn(*[flip(a) for a in lat_dirs[d]], st)
        outs_c.append(flip(o_c))
        outs_l.append(flip(o_l))
    return outs_c[0] + outs_c[1], outs_l[0] + outs_l[1]


def mixer_ab(h_ctx, h_lat, w_in, conv_w, gate_b, lb, out_g, w_out, need_ctx):
    f32 = jnp.float32
    split_idx = [int(s) for s in np.cumsum(AB_SPLIT)[:-1]]

    def prep(h):
        Bn, T = h.shape[:2]
        aq, ai, ag, aff, afb, bqk, bv, bo, bg = jnp.split(h @ w_in, split_idx, axis=-1)
        z = jnp.stack([aff, afb], 0).astype(f32).reshape(2, Bn, T, H_A, DK_A)
        lbh = lb[:, None, None]
        logf_a = jnp.log(lbh + (1 - lbh) * jax.nn.sigmoid(z))
        k_a = (1 - lbh) * jax.nn.sigmoid(-z)
        q_a = aq.reshape(Bn, T, H_A, DK_A)
        v_a = ai.reshape(Bn, T, H_A, DV_A)
        q_b, k_b = jnp.split(jax.nn.silu(conv_centred(bqk, conv_w)), 2, axis=-1)
        q_b = q_b.reshape(Bn, T, H_B, DK_B) * (DK_B ** -0.5)
        k_b = k_b.reshape(Bn, T, H_B, DK_B)
        v_b = bv.reshape(Bn, T, H_B, DV_B)
        gates = bg.astype(f32).reshape(Bn, T, 4, H_B) + gate_b.astype(f32)
        ig_b = gates[:, :, 0::2]
        logf_b = jax.nn.log_sigmoid(gates[:, :, 1::2])
        hgrn_dirs = [(q_a, k_a[d], v_a, logf_a[d]) for d in range(2)]
        mlstm_dirs = [(q_b, k_b, v_b, ig_b[:, :, d], logf_b[:, :, d]) for d in range(2)]
        gate = jnp.concatenate([jax.nn.silu(ag), jax.nn.sigmoid(bo)], axis=-1)
        return hgrn_dirs, mlstm_dirs, gate

    ha_c, hb_c, gate_c = prep(h_ctx)
    ha_l, hb_l, gate_l = prep(h_lat)
    Bn = h_lat.shape[0]
    s0_a = jnp.zeros((Bn, H_A, DK_A, DV_A), f32)
    s0_b = (jnp.zeros((Bn, H_B, DK_B, DV_B), f32), jnp.zeros((Bn, H_B, DK_B), f32),
            jnp.zeros((Bn, H_B), f32))
    oa_c, oa_l = bidirectional_prefix(hgrn2_chunk_scan, ha_c, ha_l, s0_a)
    ob_c, ob_l = bidirectional_prefix(mlstm_chunk_scan, hb_c, hb_l, s0_b)

    def out(oa, ob, gate):
        o = jnp.concatenate([oa, ob], axis=2)
        o = rmsnorm(o, out_g.reshape(H_A + H_B, DV_A)).reshape(*o.shape[:2], MIX_W)
        return (o * gate) @ w_out

    y_lat = out(oa_l, ob_l, gate_l)
    y_ctx = out(oa_c, ob_c, gate_c) if need_ctx else None
    return y_ctx, y_lat


def axial_rope_angles(T):
    rows = T // GRID_W
    r, col = jnp.meshgrid(jnp.arange(rows, dtype=jnp.float32),
                          jnp.arange(GRID_W, dtype=jnp.float32), indexing='ij')
    inv = jnp.power(ROPE_THETA, -jnp.arange(0, AXIS_DIM, 2, dtype=jnp.float32) / AXIS_DIM)
    return jnp.stack([r.reshape(-1)[:, None] * inv, col.reshape(-1)[:, None] * inv], axis=1)


def rope2d(x, ang):
    Bn, T, H, _ = x.shape
    xf = x.astype(jnp.float32).reshape(Bn, T, H, 2, AXIS_DIM)
    x1, x2 = xf[..., :AXIS_DIM // 2], xf[..., AXIS_DIM // 2:]
    cos, sin = jnp.cos(ang)[None, :, None], jnp.sin(ang)[None, :, None]
    out = jnp.concatenate([x1 * cos - x2 * sin, x2 * cos + x1 * sin], axis=-1)
    return out.reshape(Bn, T, H, HEAD_DIM).astype(x.dtype)


def attend(q, k, v):
    s = jnp.einsum('bqhgd,bkhd->bhgqk', q, k).astype(jnp.float32)
    p = jax.nn.softmax(s, axis=-1).astype(v.dtype)
    return jnp.einsum('bhgqk,bkhd->bqhgd', p, v)


def blocked_attention(q, k, v):
    Bn, T = q.shape[:2]
    qb = jnp.moveaxis(q.reshape(Bn, T // Q_BLOCK, Q_BLOCK, *q.shape[2:]), 1, 0)
    o = lax.map(lambda qi: attend(qi, k, v), qb)
    return jnp.moveaxis(o, 0, 1).reshape(Bn, T, N_HEADS_C * HEAD_DIM)


def mixer_c(h_ctx, h_lat, w_qkv, qk_g, w_out, ang, need_ctx):
    def prep(h, use_rope):
        Bn, T = h.shape[:2]
        q, k, v = jnp.split(h @ w_qkv, [N_HEADS_C * HEAD_DIM, (N_HEADS_C + N_KV_C) * HEAD_DIM], axis=-1)
        q = rmsnorm(q.reshape(Bn, T, N_HEADS_C, HEAD_DIM), qk_g[0])
        k = rmsnorm(k.reshape(Bn, T, N_KV_C, HEAD_DIM), qk_g[1])
        v = v.reshape(Bn, T, N_KV_C, HEAD_DIM)
        if use_rope:
            q, k = rope2d(q, ang), rope2d(k, ang)
        q = q.reshape(Bn, T, N_KV_C, GROUP_C, HEAD_DIM) * (HEAD_DIM ** -0.5)
        return q, k, v

    q_c, k_c, v_c = prep(h_ctx, False)
    q_l, k_l, v_l = prep(h_lat, True)
    k_all = jnp.concatenate([k_c, k_l], axis=1)
    v_all = jnp.concatenate([v_c, v_l], axis=1)
    y_lat = blocked_attention(q_l, k_all, v_all) @ w_out
    y_ctx = None
    if need_ctx:
        Bn, Tc = h_ctx.shape[:2]
        y_ctx = attend(q_c, k_c, v_c).reshape(Bn, Tc, N_HEADS_C * HEAD_DIM) @ w_out
    return y_ctx, y_lat


def setup_inputs(seed: int = 0) -> dict:
    key = jax.random.key(seed)
    ks = jax.random.split(key, 20)
    f32 = jnp.float32

    def nrm(k, shape, s):
        return jax.random.normal(k, shape, f32) * s

    ig_b = nrm(ks[11], (N_EVEN, 2, H_B), 0.01)
    fg_b = jnp.linspace(3.0, 6.0, H_B, dtype=f32) + nrm(ks[12], (N_EVEN, 2, H_B), 0.01)
    return {
        "x": nrm(ks[0], (BATCH, SEQ, D_MODEL), 1.0),
        "c": nrm(ks[1], (BATCH, D_MODEL), 1.0),
        "ctx": nrm(ks[2], (BATCH, CTX_LEN, D_MODEL), 1.0),
        "c_ctx": nrm(ks[3], (D_MODEL,), 1.0),
        "ada_w": nrm(ks[4], (DEPTH, D_MODEL, 6 * D_MODEL), 0.5 * D_MODEL ** -0.5),
        "ada_b": nrm(ks[5], (DEPTH, 6 * D_MODEL), 0.02),
        "norm_g": 1.0 + nrm(ks[6], (DEPTH, 4, D_MODEL), 0.02),
        "ffn_w_in": nrm(ks[7], (DEPTH, D_MODEL, 2 * D_FF), D_MODEL ** -0.5),
        "ffn_w_out": nrm(ks[8], (DEPTH, D_FF, D_MODEL), D_FF ** -0.5),
        "ab_w_in": nrm(ks[9], (N_EVEN, D_MODEL, D_AB_IN), D_MODEL ** -0.5),
        "ab_conv": nrm(ks[10], (N_EVEN, CONV_W, 2 * H_B * DK_B), CONV_W ** -0.5),
        "ab_gate_b": jnp.stack([ig_b, fg_b], axis=2).reshape(N_EVEN, 4, H_B),
        "hgrn_lb": nrm(ks[13], (2, DEPTH + 1, H_A * DK_A), 0.1),
        "ab_out_g": 1.0 + nrm(ks[14], (N_EVEN, MIX_W), 0.02),
        "ab_w_out": nrm(ks[15], (N_EVEN, MIX_W, D_MODEL), MIX_W ** -0.5),
        "attn_w_qkv": nrm(ks[16], (N_ODD, D_MODEL, D_QKV_C), D_MODEL ** -0.5),
        "attn_qk_g": 1.0 + nrm(ks[17], (N_ODD, 2, HEAD_DIM), 0.02),
        "attn_w_out": nrm(ks[18], (N_ODD, N_HEADS_C * HEAD_DIM, D_MODEL), (N_HEADS_C * HEAD_DIM) ** -0.5),
    }


def reference(x, c, ctx, c_ctx, ada_w, ada_b, norm_g, ffn_w_in, ffn_w_out, ab_w_in, ab_conv,
              ab_gate_b, hgrn_lb, ab_out_g, ab_w_out, attn_w_qkv, attn_qk_g, attn_w_out):
    T = x.shape[1]
    ang = axial_rope_angles(T)
    lb_all = jnp.cumsum(jax.nn.softmax(hgrn_lb.astype(jnp.float32), axis=1), axis=1)
    for l in range(DEPTH):
        need_ctx = l < DEPTH - 1
        sh1, sc1, g1, sh2, sc2, g2 = [a[:, None, :] for a in ada_params(c, ada_w[l], ada_b[l])]
        csh1, csc1, cg1, csh2, csc2, cg2 = ada_params(c_ctx, ada_w[l], ada_b[l])
        h_lat = modulate(x, norm_g[l, 0], sh1, sc1)
        h_ctx = modulate(ctx, norm_g[l, 0], csh1, csc1)
        if l % 2 == 0:
            e = l // 2
            y_ctx, y_lat = mixer_ab(h_ctx, h_lat, ab_w_in[e], ab_conv[e], ab_gate_b[e],
                                    lb_all[:, l].reshape(2, H_A, DK_A), ab_out_g[e], ab_w_out[e],
                                    need_ctx)
        else:
            o = l // 2
            y_ctx, y_lat = mixer_c(h_ctx, h_lat, attn_w_qkv[o], attn_qk_g[o], attn_w_out[o], ang,
                                   need_ctx)
        x = x + g1 * rmsnorm(y_lat, norm_g[l, 1])
        x = x + g2 * rmsnorm(swiglu(modulate(x, norm_g[l, 2], sh2, sc2), ffn_w_in[l], ffn_w_out[l]),
                             norm_g[l, 3])
        if need_ctx:
            ctx = ctx + cg1 * rmsnorm(y_ctx, norm_g[l, 1])
            ctx = ctx + cg2 * rmsnorm(swiglu(modulate(ctx, norm_g[l, 2], csh2, csc2), ffn_w_in[l],
                                             ffn_w_out[l]), norm_g[l, 3])
    return x
```

```python
import functools

import jax
import jax.numpy as jnp
from jax import lax
from jax.experimental import pallas as pl
from jax.experimental.pallas import tpu as pltpu

F32 = jnp.float32
BF16 = jnp.bfloat16
EPS = 1e-6

H_A, DK_A, DV_A = 4, 128, 128
H_B, DK_B, DV_B = 4, 64, 128
N_HEADS_C, N_KV_C, HEAD_DIM = 8, 2, 128
GROUP_C = N_HEADS_C // N_KV_C
GRID_W = 64
ROPE_THETA = 10000.0
HGRN_CHUNK = 32

LANES = 128
SCAN_BLOCK = 128
ROW_TILE = 512
ATTN_TQ = 256
ATTN_TK = 256
VMEM_LIMIT = 56 * 1024 * 1024

_COL_AQ, _COL_AI, _COL_AG, _COL_AFF, _COL_AFB = 0, 4, 8, 12, 16
_COL_BQ, _COL_BK, _COL_BV, _COL_BO = 20, 22, 24, 28
AB_MAIN = 32 * LANES


def _dot(a, b):
    return jnp.dot(a, b, preferred_element_type=F32)


def _dot_nt(a, b):
    return lax.dot_general(a, b, (((1,), (1,)), ((), ())), preferred_element_type=F32)


def _dot_tn(a, b):
    return lax.dot_general(a, b, (((0,), (0,)), ((), ())), preferred_element_type=F32)


def _sigmoid(x):
    return 1.0 / (1.0 + jnp.exp(-x))


def _log_sigmoid(x):
    return jnp.minimum(x, 0.0) - jnp.log(1.0 + jnp.exp(-jnp.abs(x)))


def _rms(x):
    return x * lax.rsqrt(jnp.mean(x * x, axis=-1, keepdims=True) + EPS)


def _modulated(x, g, mod, si):
    return _rms(x) * g * (1.0 + mod[si + 1:si + 2]) + mod[si:si + 1]


def _split3(x):
    hi = x.astype(BF16)
    r = x - hi.astype(F32)
    mid = r.astype(BF16)
    lo = (r - mid.astype(F32)).astype(BF16)
    return hi, mid, lo


def _params(sem):
    return pltpu.CompilerParams(dimension_semantics=sem, vmem_limit_bytes=VMEM_LIMIT)


def _ada_kernel(c_ref, w_ref, b_ref, o_ref):
    c = c_ref[...]
    s = c * _sigmoid(c)
    o_ref[0] = jnp.dot(s, w_ref[0], preferred_element_type=F32,
                       precision=lax.Precision.HIGHEST) + b_ref[0]


def _ada(cvecs, ada_w, ada_b):
    depth, d, n = ada_w.shape
    tn = 1024
    return pl.pallas_call(
        _ada_kernel,
        grid=(depth, n // tn),
        in_specs=[pl.BlockSpec((8, d), lambda l, j: (0, 0)),
                  pl.BlockSpec((1, d, tn), lambda l, j: (l, 0, j)),
                  pl.BlockSpec((1, 1, tn), lambda l, j: (l, 0, j))],
        out_specs=pl.BlockSpec((1, 8, tn), lambda l, j: (l, 0, j)),
        out_shape=jax.ShapeDtypeStruct((depth, 8, n), F32),
        compiler_params=_params(("arbitrary", "arbitrary")),
        name="ada",
    )(cvecs, ada_w, ada_b.reshape(depth, 1, n))


def _row_call(kernel, rows, tiles_per_mod, row_ins, mod, consts, outs, name):
    nt = rows // ROW_TILE
    in_specs = []
    args = []
    for arr, width, cb in row_ins:
        in_specs.append(pl.BlockSpec((ROW_TILE, width), lambda i, cb=cb: (i, cb)))
        args.append(arr)
    in_specs.append(pl.BlockSpec((1,) + mod.shape[1:], lambda i: (i // tiles_per_mod, 0, 0)))
    args.append(mod)
    for cst in consts:
        in_specs.append(pl.BlockSpec(cst.shape, lambda i, nd=cst.ndim: (0,) * nd))
        args.append(cst)
    out_specs = [pl.BlockSpec((ROW_TILE, w), lambda i: (i, 0)) for w, _ in outs]
    out_shape = [jax.ShapeDtypeStruct((rows, w), dt) for w, dt in outs]
    return pl.pallas_call(
        kernel, grid=(nt,), in_specs=in_specs, out_specs=out_specs, out_shape=out_shape,
        compiler_params=_params(("arbitrary",)), name=name)(*args)


def _proj_ab_kernel(x_ref, mod_ref, g_ref, w_ref, wg_ref, gb_ref, main_ref, gate_ref):
    h = _modulated(x_ref[...], g_ref[...], mod_ref[0], 0).astype(BF16)
    main_ref[...] = _dot(h, w_ref[...])
    gate_ref[...] = _dot(h, wg_ref[...]) + gb_ref[...]


def _hgrn_block(q, v, z, lb, st, tri, tri3, rev):
    one_m_lb = 1.0 - lb
    logf = jnp.log(lb + one_m_lb * _sigmoid(z))
    k = one_m_lb * _sigmoid(-z)
    hi, mid, lo = _split3(logf)
    b = _dot(tri3, jnp.concatenate([hi, mid, lo], axis=0))
    nch = SCAN_BLOCK // HGRN_CHUNK
    half = HGRN_CHUNK // 2
    bmid, blast, blast_rows = [], [], []
    for c in range(nch):
        bc = b[c * HGRN_CHUNK:(c + 1) * HGRN_CHUNK]
        r_mid = half if rev else half - 1
        r_last = 0 if rev else HGRN_CHUNK - 1
        bmid.append(jnp.broadcast_to(bc[r_mid:r_mid + 1], bc.shape))
        blast_rows.append(bc[r_last:r_last + 1])
        blast.append(jnp.broadcast_to(bc[r_last:r_last + 1], bc.shape))
    bmid = jnp.concatenate(bmid, axis=0)
    blast = jnp.concatenate(blast, axis=0)
    e1 = b - bmid
    qt = (q * jnp.exp(e1)).astype(BF16)
    kt = (k * jnp.exp(-e1)).astype(BF16)
    qh = (q * jnp.exp(b)).astype(BF16)
    kh = (k * jnp.exp(blast - b)).astype(BF16)
    vb = v.astype(BF16)
    a = jnp.where(tri, _dot_nt(qt, kt), 0.0).astype(BF16)
    o_intra = _dot(a, vb)
    outs = [None] * nch
    for c in (range(nch - 1, -1, -1) if rev else range(nch)):
        sl = slice(c * HGRN_CHUNK, (c + 1) * HGRN_CHUNK)
        outs[c] = o_intra[sl] + _dot_nt(qh[sl], st.astype(BF16))
        st = st * jnp.exp(blast_rows[c]) + _dot_tn(vb[sl], kh[sl])
    return jnp.concatenate(outs, axis=0), st


def _chunk_tri(rev):
    r = lax.broadcasted_iota(jnp.int32, (SCAN_BLOCK, SCAN_BLOCK), 0)
    c = lax.broadcasted_iota(jnp.int32, (SCAN_BLOCK, SCAN_BLOCK), 1)
    shift = HGRN_CHUNK.bit_length() - 1
    same = lax.shift_right_logical(r, shift) == lax.shift_right_logical(c, shift)
    return same & ((c >= r) if rev else (c <= r))


def _tri3(tri):
    t = jnp.where(tri, 1.0, 0.0).astype(BF16)
    return jnp.concatenate([t, t, t], axis=1)


def _hgrn_kernel(lb_ref, qc, vc, zfc, zbc, ql, vl, zfl, zbl, oc_ref, ol_ref, sc_c, sc_l, *, layer):
    raw = lb_ref[0]
    lbs = []
    for d in range(2):
        x = raw[d]
        e = jnp.exp(x - jnp.max(x, axis=0, keepdims=True))
        lbs.append(jnp.sum(e[:layer + 1], axis=0, keepdims=True) / jnp.sum(e, axis=0, keepdims=True))
    tri_f, tri_b = _chunk_tri(False), _chunk_tri(True)
    tri3_f, tri3_b = _tri3(tri_f), _tri3(tri_b)

    def run(q, v, zf, zb, o_ref, scr, carry):
        nb = q.shape[1] // SCAN_BLOCK

        def body(i, carry):
            st_f, st_b = carry
            rf = pl.ds(pl.multiple_of(i * SCAN_BLOCK, SCAN_BLOCK), SCAN_BLOCK)
            rb = pl.ds(pl.multiple_of((nb - 1 - i) * SCAN_BLOCK, SCAN_BLOCK), SCAN_BLOCK)
            o_f, st_f = _hgrn_block(q[0, rf, :], v[0, rf, :], zf[0, rf, :], lbs[0], st_f,
                                    tri_f, tri3_f, False)
            o_ref[0, rf, :] = o_f
            o_b, st_b = _hgrn_block(q[0, rb, :], v[0, rb, :], zb[0, rb, :], lbs[1], st_b,
                                    tri_b, tri3_b, True)
            scr[rb, :] = o_b
            return st_f, st_b

        carry = lax.fori_loop(0, nb, body, carry)

        def add(i, _):
            r = pl.ds(pl.multiple_of(i * SCAN_BLOCK, SCAN_BLOCK), SCAN_BLOCK)
            o_ref[0, r, :] = o_ref[0, r, :] + scr[r, :]
            return 0

        lax.fori_loop(0, nb, add, 0)
        return carry

    zero = jnp.zeros((DV_A, DK_A), F32)
    carry = run(qc, vc, zfc, zbc, oc_ref, sc_c, (zero, zero))
    run(ql, vl, zfl, zbl, ol_ref, sc_l, carry)


def _hgrn(main_c, main_l, hgrn_lb, layer):
    bsz, tc, _ = main_c.shape
    tl = main_l.shape[1]
    nl = hgrn_lb.shape[1]
    lb = hgrn_lb.reshape(2, nl, H_A, DK_A).transpose(2, 0, 1, 3)

    def col(t, cb):
        return pl.BlockSpec((1, t, LANES), lambda b, h, cb=cb: (b, 0, cb + h))

    in_specs = [pl.BlockSpec((1, 2, nl, DK_A), lambda b, h: (h, 0, 0, 0))]
    in_specs += [col(tc, c) for c in (_COL_AQ, _COL_AI, _COL_AFF, _COL_AFB)]
    in_specs += [col(tl, c) for c in (_COL_AQ, _COL_AI, _COL_AFF, _COL_AFB)]
    return pl.pallas_call(
        functools.partial(_hgrn_kernel, layer=layer),
        grid=(bsz, H_A),
        in_specs=in_specs,
        out_specs=[col(tc, 0), col(tl, 0)],
        out_shape=[jax.ShapeDtypeStruct((bsz, tc, H_A * DV_A), F32),
                   jax.ShapeDtypeStruct((bsz, tl, H_A * DV_A), F32)],
        scratch_shapes=[pltpu.VMEM((tc, LANES), F32), pltpu.VMEM((tl, LANES), F32)],
        compiler_params=_params(("arbitrary", "arbitrary")),
        name="hgrn2_scan",
    )(lb, main_c, main_c, main_c, main_c, main_l, main_l, main_l, main_l)


def _conv_silu(ref, i, nb, w):
    t = nb * SCAN_BLOCK
    r0 = pl.multiple_of(i * SCAN_BLOCK, SCAN_BLOCK)
    x = ref[0, pl.ds(r0, SCAN_BLOCK), :]
    p0 = pl.multiple_of(jnp.maximum(r0 - 8, 0), 8)
    n0 = pl.multiple_of(jnp.minimum(r0 + SCAN_BLOCK, t - 8), 8)
    prev = ref[0, pl.ds(p0, 8), :][7:8] * jnp.where(i > 0, 1.0, 0.0)
    nxt = ref[0, pl.ds(n0, 8), :][0:1] * jnp.where(i < nb - 1, 1.0, 0.0)
    row = lax.broadcasted_iota(jnp.int32, x.shape, 0)
    xp = jnp.where(row == 0, prev, pltpu.roll(x, 1, 0))
    xn = jnp.where(row == SCAN_BLOCK - 1, nxt, pltpu.roll(x, SCAN_BLOCK - 1, 0))
    y = w[0:1] * xp + w[1:2] * x + w[2:3] * xn
    return y * _sigmoid(y)


def _mlstm_chain(qh, k2, vh, bcol, brow, igcol, igrow, blast, mask, state):
    ct, n, m = state
    d = jnp.where(mask, bcol - brow + igrow, -jnp.inf)
    m_inter = bcol + m
    m_t = jnp.maximum(jnp.max(d, axis=1, keepdims=True), m_inter)
    qb = qh.astype(BF16)
    s = _dot_nt(qb, k2.astype(BF16)) * jnp.exp(d - m_t)
    w_inter = jnp.exp(m_inter - m_t)
    vb = vh.astype(BF16)
    num = _dot(s.astype(BF16), vb) + w_inter * _dot_nt(qb, ct.astype(BF16))
    den = jnp.sum(s, axis=1, keepdims=True) + w_inter * jnp.sum(qh * n, axis=1, keepdims=True)
    h = num / jnp.maximum(jnp.abs(den), jnp.exp(-m_t))
    g = blast - bcol + igcol
    m_new = jnp.maximum(blast + m, jnp.max(g, axis=0, keepdims=True))
    kw = k2 * jnp.exp(g - m_new)
    decay = jnp.exp(blast + m - m_new)
    ct = decay * ct + _dot_tn(vb, kw.astype(BF16))
    n = decay * n + jnp.sum(kw, axis=0, keepdims=True)
    return h, (ct, n, m_new)


def _mlstm_kernel(wq_ref, wk_ref, qc, kc, vc, gcc, grc, ql, kl, vl, gcl, grl,
                  oc_ref, ol_ref, sc_c, sc_l):
    p = pl.program_id(1)
    wq, wk = wq_ref[...], wk_ref[...]
    r = lax.broadcasted_iota(jnp.int32, (SCAN_BLOCK, SCAN_BLOCK), 0)
    c = lax.broadcasted_iota(jnp.int32, (SCAN_BLOCK, SCAN_BLOCK), 1)
    low, upp = c <= r, c >= r
    low_b = jnp.where(low, 1.0, 0.0).astype(BF16)
    upp_b = jnp.where(upp, 1.0, 0.0).astype(BF16)
    low3h, upp3h = jnp.concatenate([low_b] * 3, axis=1), jnp.concatenate([upp_b] * 3, axis=1)
    low3v, upp3v = jnp.concatenate([low_b] * 3, axis=0), jnp.concatenate([upp_b] * 3, axis=0)
    lane = lax.broadcasted_iota(jnp.int32, (SCAN_BLOCK, LANES), 1)
    head_lanes = [lane < DK_B, lane >= DK_B]

    def one_dir(qr, kr, vr, gcr, grr, i, nb, states, rev):
        rows = pl.ds(pl.multiple_of(i * SCAN_BLOCK, SCAN_BLOCK), SCAN_BLOCK)
        q2 = _conv_silu(qr, i, nb, wq) * (DK_B ** -0.5)
        k2 = _conv_silu(kr, i, nb, wk)
        v2 = vr[0, rows, :]
        gcol = gcr[0, rows, :]
        grow = grr[0, :, rows]
        ch, cm, cl = _split3(_log_sigmoid(gcol))
        bcol_all = _dot(upp3h if rev else low3h, jnp.concatenate([ch, cm, cl], axis=0))
        rh, rm, rl = _split3(_log_sigmoid(grow))
        brow_all = _dot(jnp.concatenate([rh, rm, rl], axis=1), low3v if rev else upp3v)
        mask = upp if rev else low
        last = 0 if rev else SCAN_BLOCK - 1
        outs, new_states = [], []
        for hl in range(2):
            ji = (2 * (1 if rev else 0)) * H_B + 2 * p + hl
            jf = ji + H_B
            sel_c = lane == jf
            bcol = jnp.sum(jnp.where(sel_c, bcol_all, 0.0), axis=1, keepdims=True)
            igcol = jnp.sum(jnp.where(lane == ji, gcol, 0.0), axis=1, keepdims=True)
            rsel = lax.broadcasted_iota(jnp.int32, brow_all.shape, 0)
            brow = jnp.sum(jnp.where(rsel == jf, brow_all, 0.0), axis=0, keepdims=True)
            igrow = jnp.sum(jnp.where(rsel == ji, grow, 0.0), axis=0, keepdims=True)
            blast = bcol[last:last + 1]
            qh = jnp.where(head_lanes[hl], q2, 0.0)
            h, st = _mlstm_chain(qh, k2, v2[:, hl * DV_B:(hl + 1) * DV_B], bcol, brow, igcol, igrow,
                                 blast, mask, states[hl])
            outs.append(h)
            new_states.append(st)
        return jnp.concatenate(outs, axis=1), tuple(new_states)

    def run(qr, kr, vr, gcr, grr, o_ref, scr, carry):
        nb = qr.shape[1] // SCAN_BLOCK

        def body(i, carry):
            st_f, st_b = carry
            o_f, st_f = one_dir(qr, kr, vr, gcr, grr, i, nb, st_f, False)
            o_ref[0, pl.ds(pl.multiple_of(i * SCAN_BLOCK, SCAN_BLOCK), SCAN_BLOCK), :] = o_f
            j = nb - 1 - i
            o_b, st_b = one_dir(qr, kr, vr, gcr, grr, j, nb, st_b, True)
            scr[pl.ds(pl.multiple_of(j * SCAN_BLOCK, SCAN_BLOCK), SCAN_BLOCK), :] = o_b
            return st_f, st_b

        carry = lax.fori_loop(0, nb, body, carry)

        def add(i, _):
            rr = pl.ds(pl.multiple_of(i * SCAN_BLOCK, SCAN_BLOCK), SCAN_BLOCK)
            o_ref[0, rr, :] = o_ref[0, rr, :] + scr[rr, :]
            return 0

        lax.fori_loop(0, nb, add, 0)
        return carry

    s0 = (jnp.zeros((DV_B, LANES), F32), jnp.zeros((1, LANES), F32), jnp.zeros((1, 1), F32))
    carry = run(qc, kc, vc, gcc, grc, oc_ref, sc_c, ((s0, s0), (s0, s0)))
    run(ql, kl, vl, gcl, grl, ol_ref, sc_l, carry)


def _mlstm(main_c, main_l, gate_c, gate_l, conv_w):
    bsz, tc, _ = main_c.shape
    tl = main_l.shape[1]
    ng = 4 * H_B
    grow_c = jnp.swapaxes(gate_c[:, :, :ng], 1, 2)
    grow_l = jnp.swapaxes(gate_l[:, :, :ng], 1, 2)

    def col(t, cb, w=LANES):
        return pl.BlockSpec((1, t, w), lambda b, p, cb=cb: (b, 0, cb + p))

    def seg(t):
        return [col(t, _COL_BQ), col(t, _COL_BK), col(t, _COL_BV // 2, 2 * LANES),
                pl.BlockSpec((1, t, LANES), lambda b, p: (b, 0, 0)),
                pl.BlockSpec((1, ng, t), lambda b, p: (b, 0, 0))]

    in_specs = [pl.BlockSpec((3, LANES), lambda b, p: (0, p)),
                pl.BlockSpec((3, LANES), lambda b, p: (0, 2 + p))] + seg(tc) + seg(tl)
    return pl.pallas_call(
        _mlstm_kernel,
        grid=(bsz, H_B // 2),
        in_specs=in_specs,
        out_specs=[col(tc, 0, 2 * LANES), col(tl, 0, 2 * LANES)],
        out_shape=[jax.ShapeDtypeStruct((bsz, tc, H_B * DV_B), F32),
                   jax.ShapeDtypeStruct((bsz, tl, H_B * DV_B), F32)],
        scratch_shapes=[pltpu.VMEM((tc, 2 * LANES), F32), pltpu.VMEM((tl, 2 * LANES), F32)],
        compiler_params=_params(("arbitrary", "arbitrary")),
        name="mlstm_scan",
    )(conv_w, conv_w, main_c, main_c, main_c, gate_c, grow_c, main_l, main_l, main_l, gate_l, grow_l)


def _mix_out_kernel(oa_ref, ob_ref, ag_ref, bo_ref, x_ref, mod_ref, og_ref, w_ref, ng_ref, xo_ref):
    og = og_ref[...]
    pieces = []
    for h in range(H_A):
        sl = slice(h * LANES, (h + 1) * LANES)
        a = ag_ref[:, sl]
        pieces.append((_rms(oa_ref[:, sl]) * og[:, sl] * (a * _sigmoid(a))).astype(BF16))
    for h in range(H_B):
        sl = slice(h * LANES, (h + 1) * LANES)
        gsl = slice((H_A + h) * LANES, (H_A + h + 1) * LANES)
        pieces.append((_rms(ob_ref[:, sl]) * og[:, gsl] * _sigmoid(bo_ref[:, sl])).astype(BF16))
    y = _dot(jnp.concatenate(pieces, axis=1), w_ref[...])
    xo_ref[...] = x_ref[...] + mod_ref[0][2:3] * (_rms(y) * ng_ref[...])


def _ffn_kernel(x_ref, mod_ref, g2_ref, g3_ref, win_ref, wout_ref, xo_ref, *, d_ff, chunk):
    x = x_ref[...]
    mod = mod_ref[0]
    h = _modulated(x, g2_ref[...], mod, 3).astype(BF16)
    acc = None
    for j in range(d_ff // chunk):
        g = _dot(h, win_ref[:, j * chunk:(j + 1) * chunk])
        u = _dot(h, win_ref[:, d_ff + j * chunk:d_ff + (j + 1) * chunk])
        a = (g * _sigmoid(g) * u).astype(BF16)
        part = _dot(a, wout_ref[j * chunk:(j + 1) * chunk, :])
        acc = part if acc is None else acc + part
    xo_ref[...] = x + mod[5:6] * (_rms(acc) * g3_ref[...])


def _ffn(x2d, tiles_per_mod, mod, g2, g3, w_in, w_out):
    d_ff = w_out.shape[0]
    chunk = d_ff // 2 if (d_ff // 2) % LANES == 0 else d_ff
    (out,) = _row_call(functools.partial(_ffn_kernel, d_ff=d_ff, chunk=chunk), x2d.shape[0], tiles_per_mod,
                       [(x2d, x2d.shape[1], 0)], mod, [g2, g3, w_in, w_out],
                       [(x2d.shape[1], F32)], "swiglu")
    return out


def _rope_rot(t):
    lane = lax.broadcasted_iota(jnp.int32, t.shape, 1)
    first = (lane & (HEAD_DIM // 2 - 1)) < HEAD_DIM // 4
    return jnp.where(first, pltpu.roll(t, LANES - HEAD_DIM // 4, 1), pltpu.roll(t, HEAD_DIM // 4, 1))


def _qkv_kernel(*refs, use_rope):
    if use_rope:
        x_ref, cos_ref, sin_ref, mod_ref, g_ref, w_ref, qkg_ref, q_ref, k_ref, v_ref = refs
        cos, sin = cos_ref[...], sin_ref[...]
    else:
        x_ref, mod_ref, g_ref, w_ref, qkg_ref, q_ref, k_ref, v_ref = refs
    h = _modulated(x_ref[...], g_ref[...], mod_ref[0], 0).astype(BF16)
    qkv = _dot(h, w_ref[...])
    qkg = qkg_ref[...]

    def head(j, g):
        t = _rms(qkv[:, j * LANES:(j + 1) * LANES]) * g
        if use_rope:
            t = t * cos + _rope_rot(t) * sin
        return t

    for j in range(N_HEADS_C):
        q_ref[:, j * LANES:(j + 1) * LANES] = (head(j, qkg[0:1]) * (HEAD_DIM ** -0.5)).astype(BF16)
    for j in range(N_KV_C):
        k_ref[:, j * LANES:(j + 1) * LANES] = head(N_HEADS_C + j, qkg[1:2]).astype(BF16)
    v0 = (N_HEADS_C + N_KV_C) * LANES
    v_ref[...] = qkv[:, v0:v0 + N_KV_C * LANES].astype(BF16)


def _rope_tables(t):
    axis_dim = HEAD_DIM // 2
    rows = t // GRID_W
    r, col = jnp.meshgrid(jnp.arange(rows, dtype=F32), jnp.arange(GRID_W, dtype=F32), indexing='ij')
    inv = jnp.power(ROPE_THETA, -jnp.arange(0, axis_dim, 2, dtype=F32) / axis_dim)
    ar = r.reshape(-1)[:, None] * inv
    ac = col.reshape(-1)[:, None] * inv
    cos = jnp.concatenate([jnp.cos(ar), jnp.cos(ar), jnp.cos(ac), jnp.cos(ac)], axis=1)
    sin = jnp.concatenate([-jnp.sin(ar), jnp.sin(ar), -jnp.sin(ac), jnp.sin(ac)], axis=1)
    return cos, sin


def _attn_kernel(q_ref, k_ref, v_ref, o_ref, m_scr, l_scr, acc_scr):
    kj = pl.program_id(3)

    @pl.when(kj == 0)
    def _():
        m_scr[...] = jnp.full(m_scr.shape, -jnp.inf, F32)
        l_scr[...] = jnp.zeros(l_scr.shape, F32)
        acc_scr[...] = jnp.zeros(acc_scr.shape, F32)

    q = jnp.concatenate([q_ref[0, :, g * LANES:(g + 1) * LANES] for g in range(GROUP_C)], axis=0)
    s = _dot_nt(q, k_ref[0])
    m_prev = m_scr[...]
    m_new = jnp.maximum(m_prev, jnp.max(s, axis=1, keepdims=True))
    alpha = jnp.exp(m_prev - m_new)
    p = jnp.exp(s - m_new)
    l_scr[...] = alpha * l_scr[...] + jnp.sum(p, axis=1, keepdims=True)
    acc_scr[...] = alpha * acc_scr[...] + _dot(p.astype(BF16), v_ref[0])
    m_scr[...] = m_new

    @pl.when(kj == pl.num_programs(3) - 1)
    def _():
        o = acc_scr[...] / l_scr[...]
        tq = o_ref.shape[1]
        for g in range(GROUP_C):
            o_ref[0, :, g * LANES:(g + 1) * LANES] = o[g * tq:(g + 1) * tq].astype(o_ref.dtype)


def _attention(q, k, v):
    bsz, tq_all, _ = q.shape
    tk_all = k.shape[1]
    gw = GROUP_C * HEAD_DIM
    return pl.pallas_call(
        _attn_kernel,
        grid=(bsz, N_KV_C, tq_all // ATTN_TQ, tk_all // ATTN_TK),
        in_specs=[pl.BlockSpec((1, ATTN_TQ, gw), lambda b, h, i, j: (b, i, h)),
                  pl.BlockSpec((1, ATTN_TK, HEAD_DIM), lambda b, h, i, j: (b, j, h)),
                  pl.BlockSpec((1, ATTN_TK, HEAD_DIM), lambda b, h, i, j: (b, j, h))],
        out_specs=pl.BlockSpec((1, ATTN_TQ, gw), lambda b, h, i, j: (b, i, h)),
        out_shape=jax.ShapeDtypeStruct(q.shape, BF16),
        scratch_shapes=[pltpu.VMEM((GROUP_C * ATTN_TQ, 1), F32), pltpu.VMEM((GROUP_C * ATTN_TQ, 1), F32),
                        pltpu.VMEM((GROUP_C * ATTN_TQ, HEAD_DIM), F32)],
        compiler_params=_params(("arbitrary", "arbitrary", "arbitrary", "arbitrary")),
        name="attention",
    )(q, k, v)


def _attn_out_kernel(o_ref, x_ref, mod_ref, w_ref, ng_ref, xo_ref):
    y = _dot(o_ref[...], w_ref[...])
    xo_ref[...] = x_ref[...] + mod_ref[0][2:3] * (_rms(y) * ng_ref[...])


def kernel(x, c, ctx, c_ctx, ada_w, ada_b, norm_g, ffn_w_in, ffn_w_out, ab_w_in, ab_conv, ab_gate_b,
           hgrn_lb, ab_out_g, ab_w_out, attn_w_qkv, attn_qk_g, attn_w_out):
    bsz, tl, d = x.shape
    tc = ctx.shape[1]
    depth = ada_w.shape[0]
    rl, rc = bsz * tl, bsz * tc
    tiles_l, tiles_c = tl // ROW_TILE, rc // ROW_TILE

    cvecs = jnp.zeros((8, d), F32).at[:bsz].set(c).at[bsz].set(c_ctx)
    ada = _ada(cvecs, ada_w, ada_b)
    xl = x.reshape(rl, d)
    xc = ctx.reshape(rc, d)

    for l in range(depth):
        need_ctx = l < depth - 1
        mod_l = ada[l, :bsz].reshape(bsz, 6, d)
        mod_c = ada[l, bsz:bsz + 1].reshape(1, 6, d)
        ng = [norm_g[l, j][None] for j in range(4)]
        segs = [(xl, tl, tiles_l, mod_l), (xc, tc, tiles_c, mod_c)]
        if l % 2 == 0:
            e = l // 2
            w = ab_w_in[e]
            w_main = w[:, :AB_MAIN].astype(BF16)
            n_gate = w.shape[1] - AB_MAIN
            w_gate = jnp.zeros((d, LANES), F32).at[:, :n_gate].set(w[:, AB_MAIN:]).astype(BF16)
            gate_b = jnp.zeros((1, LANES), F32).at[0, :n_gate].set(ab_gate_b[e].reshape(-1))
            mains, gates = [], []
            for x2d, t, tpm, mod in segs:
                main, gate = _row_call(_proj_ab_kernel, x2d.shape[0], tpm, [(x2d, d, 0)], mod,
                                       [ng[0], w_main, w_gate, gate_b],
                                       [(AB_MAIN, F32), (LANES, F32)], "proj_ab")
                mains.append(main.reshape(bsz, t, AB_MAIN))
                gates.append(gate.reshape(bsz, t, LANES))
            oa_c, oa_l = _hgrn(mains[1], mains[0], hgrn_lb, l)
            ob_c, ob_l = _mlstm(mains[1], mains[0], gates[1], gates[0], ab_conv[e])
            w_out = ab_w_out[e].astype(BF16)
            og = ab_out_g[e][None]
            new = []
            for (x2d, t, tpm, mod), main, oa, ob in zip(segs, mains, (oa_l, oa_c), (ob_l, ob_c)):
                if x2d is xc and not need_ctx:
                    new.append(x2d)
                    continue
                m2d = main.reshape(-1, AB_MAIN)
                (xo,) = _row_call(_mix_out_kernel, x2d.shape[0], tpm,
                                  [(oa.reshape(-1, H_A * DV_A), H_A * DV_A, 0),
                                   (ob.reshape(-1, H_B * DV_B), H_B * DV_B, 0),
                                   (m2d, H_A * DV_A, _COL_AG // H_A), (m2d, H_B * DV_B, _COL_BO // H_B),
                                   (x2d, d, 0)], mod, [og, w_out, ng[1]], [(d, F32)], "mix_out")
                new.append(xo)
            xl, xc = new
        else:
            o = l // 2
            w_qkv = attn_w_qkv[o].astype(BF16)
            qkg = attn_qk_g[o]
            cos, sin = _rope_tables(tl)
            cos = jnp.tile(cos, (bsz, 1))
            sin = jnp.tile(sin, (bsz, 1))
            nq, nkv = N_HEADS_C * HEAD_DIM, N_KV_C * HEAD_DIM
            outs = [(nq, BF16), (nkv, BF16), (nkv, BF16)]
            q_l, k_l, v_l = _row_call(functools.partial(_qkv_kernel, use_rope=True), rl, tiles_l,
                                      [(xl, d, 0), (cos, LANES, 0), (sin, LANES, 0)], mod_l,
                                      [ng[0], w_qkv, qkg], outs, "qkv_lat")
            q_c, k_c, v_c = _row_call(functools.partial(_qkv_kernel, use_rope=False), rc, tiles_c,
                                      [(xc, d, 0)], mod_c, [ng[0], w_qkv, qkg], outs, "qkv_ctx")
            k_all = jnp.concatenate([k_l.reshape(bsz, tl, nkv), k_c.reshape(bsz, tc, nkv)], axis=1)
            v_all = jnp.concatenate([v_l.reshape(bsz, tl, nkv), v_c.reshape(bsz, tc, nkv)], axis=1)
            w_out = attn_w_out[o].astype(BF16)
            att = _attention(q_l.reshape(bsz, tl, nq), k_all, v_all)
            (xl_new,) = _row_call(_attn_out_kernel, rl, tiles_l, [(att.reshape(rl, nq), nq, 0), (xl, d, 0)],
                                  mod_l, [w_out, ng[1]], [(d, F32)], "attn_out")
            if need_ctx:
                att_c = _attention(q_c.reshape(bsz, tc, nq), k_c.reshape(bsz, tc, nkv), v_c.reshape(bsz, tc, nkv))
                (xc,) = _row_call(_attn_out_kernel, rc, tiles_c, [(att_c.reshape(rc, nq), nq, 0), (xc, d, 0)],
                                  mod_c, [w_out, ng[1]], [(d, F32)], "attn_out_ctx")
            xl = xl_new
        w_in = ffn_w_in[l].astype(BF16)
        w_o = ffn_w_out[l].astype(BF16)
        xl = _ffn(xl, tiles_l, mod_l, ng[2], ng[3], w_in, w_o)
        if need_ctx:
            xc = _ffn(xc, tiles_c, mod_c, ng[2], ng[3], w_in, w_o)
    return xl.reshape(bsz, tl, d)
```

```python
import functools

import jax
import jax.numpy as jnp
from jax import lax
from jax.experimental import pallas as pl
from jax.experimental.pallas import tpu as pltpu

F32 = jnp.float32
BF16 = jnp.bfloat16
EPS = 1e-6

H_A, DK_A, DV_A = 4, 128, 128
H_B, DK_B, DV_B = 4, 64, 128
N_HEADS_C, N_KV_C, HEAD_DIM = 8, 2, 128
GROUP_C = N_HEADS_C // N_KV_C
GRID_W = 64
ROPE_THETA = 10000.0
HGRN_CHUNK = 32

LANES = 128
SCAN_BLOCK = 128
ROW_TILE = 512
ATTN_TQ = 128
VMEM_LIMIT = 56 * 1024 * 1024

_COL_AQ, _COL_AI, _COL_AG, _COL_AFF, _COL_AFB = 0, 4, 8, 12, 16
_COL_BQ, _COL_BK, _COL_BV, _COL_BO = 20, 22, 24, 28
AB_MAIN = 32 * LANES


def _dot(a, b):
    return jnp.dot(a, b, preferred_element_type=F32)


def _dot_nt(a, b):
    return lax.dot_general(a, b, (((1,), (1,)), ((), ())), preferred_element_type=F32)


def _dot_tn(a, b):
    return lax.dot_general(a, b, (((0,), (0,)), ((), ())), preferred_element_type=F32)


def _sigmoid(x):
    return 1.0 / (1.0 + jnp.exp(-x))


def _log_sigmoid(x):
    return jnp.minimum(x, 0.0) - jnp.log(1.0 + jnp.exp(-jnp.abs(x)))


def _rms(x):
    return x * lax.rsqrt(jnp.mean(x * x, axis=-1, keepdims=True) + EPS)


def _modulated(x, g, mod, si):
    return _rms(x) * g * (1.0 + mod[si + 1:si + 2]) + mod[si:si + 1]


def _split3(x):
    hi = x.astype(BF16)
    r = x - hi.astype(F32)
    mid = r.astype(BF16)
    lo = (r - mid.astype(F32)).astype(BF16)
    return hi, mid, lo


def _params(sem):
    return pltpu.CompilerParams(dimension_semantics=sem, vmem_limit_bytes=VMEM_LIMIT)


def _ada_kernel(c_ref, w_ref, b_ref, o_ref):
    c = c_ref[...]
    s = c * _sigmoid(c)
    o_ref[0] = jnp.dot(s, w_ref[0], preferred_element_type=F32,
                       precision=lax.Precision.HIGHEST) + b_ref[0]


def _ada(cvecs, ada_w, ada_b):
    depth, d, n = ada_w.shape
    tn = 1024
    return pl.pallas_call(
        _ada_kernel,
        grid=(depth, n // tn),
        in_specs=[pl.BlockSpec((8, d), lambda l, j: (0, 0)),
                  pl.BlockSpec((1, d, tn), lambda l, j: (l, 0, j)),
                  pl.BlockSpec((1, 1, tn), lambda l, j: (l, 0, j))],
        out_specs=pl.BlockSpec((1, 8, tn), lambda l, j: (l, 0, j)),
        out_shape=jax.ShapeDtypeStruct((depth, 8, n), F32),
        compiler_params=_params(("arbitrary", "arbitrary")),
        name="ada",
    )(cvecs, ada_w, ada_b.reshape(depth, 1, n))


def _row_call(kernel, rows, tiles_per_mod, row_ins, mod, consts, outs, name):
    nt = rows // ROW_TILE
    in_specs = []
    args = []
    for arr, width, cb in row_ins:
        in_specs.append(pl.BlockSpec((ROW_TILE, width), lambda i, cb=cb: (i, cb)))
        args.append(arr)
    in_specs.append(pl.BlockSpec((1,) + mod.shape[1:], lambda i: (i // tiles_per_mod, 0, 0)))
    args.append(mod)
    for cst in consts:
        in_specs.append(pl.BlockSpec(cst.shape, lambda i, nd=cst.ndim: (0,) * nd))
        args.append(cst)
    out_specs = [pl.BlockSpec((ROW_TILE, w), lambda i: (i, 0)) for w, _ in outs]
    out_shape = [jax.ShapeDtypeStruct((rows, w), dt) for w, dt in outs]
    return pl.pallas_call(
        kernel, grid=(nt,), in_specs=in_specs, out_specs=out_specs, out_shape=out_shape,
        compiler_params=_params(("arbitrary",)), name=name)(*args)


def _proj_ab_kernel(x_ref, mod_ref, g_ref, w_ref, wg_ref, gb_ref, main_ref, gate_ref):
    h = _modulated(x_ref[...], g_ref[...], mod_ref[0], 0).astype(BF16)
    main_ref[...] = _dot(h, w_ref[...])
    gate_ref[...] = _dot(h, wg_ref[...]) + gb_ref[...]


def _hgrn_block(q, v, z, lb, st, tri, tri3, rev):
    one_m_lb = 1.0 - lb
    logf = jnp.log(lb + one_m_lb * _sigmoid(z))
    k = one_m_lb * _sigmoid(-z)
    hi, mid, lo = _split3(logf)
    b = _dot(tri3, jnp.concatenate([hi, mid, lo], axis=0))
    nch = SCAN_BLOCK // HGRN_CHUNK
    half = HGRN_CHUNK // 2
    bmid, blast, blast_rows = [], [], []
    for c in range(nch):
        bc = b[c * HGRN_CHUNK:(c + 1) * HGRN_CHUNK]
        r_mid = half if rev else half - 1
        r_last = 0 if rev else HGRN_CHUNK - 1
        bmid.append(jnp.broadcast_to(bc[r_mid:r_mid + 1], bc.shape))
        blast_rows.append(bc[r_last:r_last + 1])
        blast.append(jnp.broadcast_to(bc[r_last:r_last + 1], bc.shape))
    bmid = jnp.concatenate(bmid, axis=0)
    blast = jnp.concatenate(blast, axis=0)
    e1 = b - bmid
    qt = (q * jnp.exp(e1)).astype(BF16)
    kt = (k * jnp.exp(-e1)).astype(BF16)
    qh = (q * jnp.exp(b)).astype(BF16)
    kh = (k * jnp.exp(blast - b)).astype(BF16)
    vb = v.astype(BF16)
    a = jnp.where(tri, _dot_nt(qt, kt), 0.0).astype(BF16)
    o_intra = _dot(a, vb)
    outs = [None] * nch
    for c in (range(nch - 1, -1, -1) if rev else range(nch)):
        sl = slice(c * HGRN_CHUNK, (c + 1) * HGRN_CHUNK)
        outs[c] = o_intra[sl] + _dot_nt(qh[sl], st.astype(BF16))
        st = st * jnp.exp(blast_rows[c]) + _dot_tn(vb[sl], kh[sl])
    return jnp.concatenate(outs, axis=0), st


def _chunk_tri(rev):
    r = lax.broadcasted_iota(jnp.int32, (SCAN_BLOCK, SCAN_BLOCK), 0)
    c = lax.broadcasted_iota(jnp.int32, (SCAN_BLOCK, SCAN_BLOCK), 1)
    shift = HGRN_CHUNK.bit_length() - 1
    same = lax.shift_right_logical(r, shift) == lax.shift_right_logical(c, shift)
    return same & ((c >= r) if rev else (c <= r))


def _tri3(tri):
    t = jnp.where(tri, 1.0, 0.0).astype(BF16)
    return jnp.concatenate([t, t, t], axis=1)


def _hgrn_kernel(lb_ref, qc, vc, zfc, zbc, ql, vl, zfl, zbl, oc_ref, ol_ref, sc_c, sc_l, *, layer):
    raw = lb_ref[0]
    lbs = []
    for d in range(2):
        x = raw[d]
        e = jnp.exp(x - jnp.max(x, axis=0, keepdims=True))
        lbs.append(jnp.sum(e[:layer + 1], axis=0, keepdims=True) / jnp.sum(e, axis=0, keepdims=True))
    tri_f, tri_b = _chunk_tri(False), _chunk_tri(True)
    tri3_f, tri3_b = _tri3(tri_f), _tri3(tri_b)

    def run(q, v, zf, zb, o_ref, scr, carry):
        nb = q.shape[1] // SCAN_BLOCK

        def body(i, carry):
            st_f, st_b = carry
            rf = pl.ds(pl.multiple_of(i * SCAN_BLOCK, SCAN_BLOCK), SCAN_BLOCK)
            rb = pl.ds(pl.multiple_of((nb - 1 - i) * SCAN_BLOCK, SCAN_BLOCK), SCAN_BLOCK)
            o_f, st_f = _hgrn_block(q[0, rf, :], v[0, rf, :], zf[0, rf, :], lbs[0], st_f,
                                    tri_f, tri3_f, False)
            o_ref[0, rf, :] = o_f
            o_b, st_b = _hgrn_block(q[0, rb, :], v[0, rb, :], zb[0, rb, :], lbs[1], st_b,
                                    tri_b, tri3_b, True)
            scr[rb, :] = o_b
            return st_f, st_b

        carry = lax.fori_loop(0, nb, body, carry)

        def add(i, _):
            r = pl.ds(pl.multiple_of(i * SCAN_BLOCK, SCAN_BLOCK), SCAN_BLOCK)
            o_ref[0, r, :] = o_ref[0, r, :] + scr[r, :]
            return 0

        lax.fori_loop(0, nb, add, 0)
        return carry

    zero = jnp.zeros((DV_A, DK_A), F32)
    carry = run(qc, vc, zfc, zbc, oc_ref, sc_c, (zero, zero))
    run(ql, vl, zfl, zbl, ol_ref, sc_l, carry)


def _hgrn(main_c, main_l, hgrn_lb, layer):
    bsz, tc, _ = main_c.shape
    tl = main_l.shape[1]
    nl = hgrn_lb.shape[1]
    lb = hgrn_lb.reshape(2, nl, H_A, DK_A).transpose(2, 0, 1, 3)

    def col(t, cb):
        return pl.BlockSpec((1, t, LANES), lambda b, h, cb=cb: (b, 0, cb + h))

    in_specs = [pl.BlockSpec((1, 2, nl, DK_A), lambda b, h: (h, 0, 0, 0))]
    in_specs += [col(tc, c) for c in (_COL_AQ, _COL_AI, _COL_AFF, _COL_AFB)]
    in_specs += [col(tl, c) for c in (_COL_AQ, _COL_AI, _COL_AFF, _COL_AFB)]
    return pl.pallas_call(
        functools.partial(_hgrn_kernel, layer=layer),
        grid=(bsz, H_A),
        in_specs=in_specs,
        out_specs=[col(tc, 0), col(tl, 0)],
        out_shape=[jax.ShapeDtypeStruct((bsz, tc, H_A * DV_A), F32),
                   jax.ShapeDtypeStruct((bsz, tl, H_A * DV_A), F32)],
        scratch_shapes=[pltpu.VMEM((tc, LANES), F32), pltpu.VMEM((tl, LANES), F32)],
        compiler_params=_params(("arbitrary", "arbitrary")),
        name="hgrn2_scan",
    )(lb, main_c, main_c, main_c, main_c, main_l, main_l, main_l, main_l)


def _conv_silu(ref, i, nb, w):
    t = nb * SCAN_BLOCK
    r0 = pl.multiple_of(i * SCAN_BLOCK, SCAN_BLOCK)
    x = ref[0, pl.ds(r0, SCAN_BLOCK), :]
    p0 = pl.multiple_of(jnp.maximum(r0 - 8, 0), 8)
    n0 = pl.multiple_of(jnp.minimum(r0 + SCAN_BLOCK, t - 8), 8)
    prev = ref[0, pl.ds(p0, 8), :][7:8] * jnp.where(i > 0, 1.0, 0.0)
    nxt = ref[0, pl.ds(n0, 8), :][0:1] * jnp.where(i < nb - 1, 1.0, 0.0)
    row = lax.broadcasted_iota(jnp.int32, x.shape, 0)
    xp = jnp.where(row == 0, prev, pltpu.roll(x, 1, 0))
    xn = jnp.where(row == SCAN_BLOCK - 1, nxt, pltpu.roll(x, SCAN_BLOCK - 1, 0))
    y = w[0:1] * xp + w[1:2] * x + w[2:3] * xn
    return y * _sigmoid(y)


def _mlstm_chain(qh, k2, vh, bcol, brow, igcol, igrow, blast, mask, state):
    ct, n, m = state
    d = jnp.where(mask, bcol - brow + igrow, -jnp.inf)
    m_inter = bcol + m
    m_t = jnp.maximum(jnp.max(d, axis=1, keepdims=True), m_inter)
    qb = qh.astype(BF16)
    s = _dot_nt(qb, k2.astype(BF16)) * jnp.exp(d - m_t)
    w_inter = jnp.exp(m_inter - m_t)
    vb = vh.astype(BF16)
    num = _dot(s.astype(BF16), vb) + w_inter * _dot_nt(qb, ct.astype(BF16))
    den = jnp.sum(s, axis=1, keepdims=True) + w_inter * jnp.sum(qh * n, axis=1, keepdims=True)
    h = num / jnp.maximum(jnp.abs(den), jnp.exp(-m_t))
    g = blast - bcol + igcol
    m_new = jnp.maximum(blast + m, jnp.max(g, axis=0, keepdims=True))
    kw = k2 * jnp.exp(g - m_new)
    decay = jnp.exp(blast + m - m_new)
    ct = decay * ct + _dot_tn(vb, kw.astype(BF16))
    n = decay * n + jnp.sum(kw, axis=0, keepdims=True)
    return h, (ct, n, m_new)


def _mlstm_kernel(wq_ref, wk_ref, qc, kc, vc, gcc, grc, ql, kl, vl, gcl, grl,
                  oc_ref, ol_ref, sc_c, sc_l):
    p = pl.program_id(1)
    wq, wk = wq_ref[...], wk_ref[...]
    r = lax.broadcasted_iota(jnp.int32, (SCAN_BLOCK, SCAN_BLOCK), 0)
    c = lax.broadcasted_iota(jnp.int32, (SCAN_BLOCK, SCAN_BLOCK), 1)
    low, upp = c <= r, c >= r
    low_b = jnp.where(low, 1.0, 0.0).astype(BF16)
    upp_b = jnp.where(upp, 1.0, 0.0).astype(BF16)
    low3h, upp3h = jnp.concatenate([low_b] * 3, axis=1), jnp.concatenate([upp_b] * 3, axis=1)
    low3v, upp3v = jnp.concatenate([low_b] * 3, axis=0), jnp.concatenate([upp_b] * 3, axis=0)
    lane = lax.broadcasted_iota(jnp.int32, (SCAN_BLOCK, LANES), 1)
    head_lanes = [lane < DK_B, lane >= DK_B]

    def one_dir(qr, kr, vr, gcr, grr, i, nb, states, rev):
        rows = pl.ds(pl.multiple_of(i * SCAN_BLOCK, SCAN_BLOCK), SCAN_BLOCK)
        q2 = _conv_silu(qr, i, nb, wq) * (DK_B ** -0.5)
        k2 = _conv_silu(kr, i, nb, wk)
        v2 = vr[0, rows, :]
        gcol = gcr[0, rows, :]
        grow = grr[0, :, rows]
        ch, cm, cl = _split3(_log_sigmoid(gcol))
        bcol_all = _dot(upp3h if rev else low3h, jnp.concatenate([ch, cm, cl], axis=0))
        rh, rm, rl = _split3(_log_sigmoid(grow))
        brow_all = _dot(jnp.concatenate([rh, rm, rl], axis=1), low3v if rev else upp3v)
        mask = upp if rev else low
        last = 0 if rev else SCAN_BLOCK - 1
        outs, new_states = [], []
        for hl in range(2):
            ji = (2 * (1 if rev else 0)) * H_B + 2 * p + hl
            jf = ji + H_B
            sel_c = lane == jf
            bcol = jnp.sum(jnp.where(sel_c, bcol_all, 0.0), axis=1, keepdims=True)
            igcol = jnp.sum(jnp.where(lane == ji, gcol, 0.0), axis=1, keepdims=True)
            rsel = lax.broadcasted_iota(jnp.int32, brow_all.shape, 0)
            brow = jnp.sum(jnp.where(rsel == jf, brow_all, 0.0), axis=0, keepdims=True)
            igrow = jnp.sum(jnp.where(rsel == ji, grow, 0.0), axis=0, keepdims=True)
            blast = bcol[last:last + 1]
            qh = jnp.where(head_lanes[hl], q2, 0.0)
            h, st = _mlstm_chain(qh, k2, v2[:, hl * DV_B:(hl + 1) * DV_B], bcol, brow, igcol, igrow,
                                 blast, mask, states[hl])
            outs.append(h)
            new_states.append(st)
        return jnp.concatenate(outs, axis=1), tuple(new_states)

    def run(qr, kr, vr, gcr, grr, o_ref, scr, carry):
        nb = qr.shape[1] // SCAN_BLOCK

        def body(i, carry):
            st_f, st_b = carry
            o_f, st_f = one_dir(qr, kr, vr, gcr, grr, i, nb, st_f, False)
            o_ref[0, pl.ds(pl.multiple_of(i * SCAN_BLOCK, SCAN_BLOCK), SCAN_BLOCK), :] = o_f
            j = nb - 1 - i
            o_b, st_b = one_dir(qr, kr, vr, gcr, grr, j, nb, st_b, True)
            scr[pl.ds(pl.multiple_of(j * SCAN_BLOCK, SCAN_BLOCK), SCAN_BLOCK), :] = o_b
            return st_f, st_b

        carry = lax.fori_loop(0, nb, body, carry)

        def add(i, _):
            rr = pl.ds(pl.multiple_of(i * SCAN_BLOCK, SCAN_BLOCK), SCAN_BLOCK)
            o_ref[0, rr, :] = o_ref[0, rr, :] + scr[rr, :]
            return 0

        lax.fori_loop(0, nb, add, 0)
        return carry

    s0 = (jnp.zeros((DV_B, LANES), F32), jnp.zeros((1, LANES), F32), jnp.zeros((1, 1), F32))
    carry = run(qc, kc, vc, gcc, grc, oc_ref, sc_c, ((s0, s0), (s0, s0)))
    run(ql, kl, vl, gcl, grl, ol_ref, sc_l, carry)


def _mlstm(main_c, main_l, gate_c, gate_l, conv_w):
    bsz, tc, _ = main_c.shape
    tl = main_l.shape[1]
    ng = 4 * H_B
    grow_c = jnp.swapaxes(gate_c[:, :, :ng], 1, 2)
    grow_l = jnp.swapaxes(gate_l[:, :, :ng], 1, 2)

    def col(t, cb, w=LANES):
        return pl.BlockSpec((1, t, w), lambda b, p, cb=cb: (b, 0, cb + p))

    def seg(t):
        return [col(t, _COL_BQ), col(t, _COL_BK), col(t, _COL_BV // 2, 2 * LANES),
                pl.BlockSpec((1, t, LANES), lambda b, p: (b, 0, 0)),
                pl.BlockSpec((1, ng, t), lambda b, p: (b, 0, 0))]

    in_specs = [pl.BlockSpec((3, LANES), lambda b, p: (0, p)),
                pl.BlockSpec((3, LANES), lambda b, p: (0, 2 + p))] + seg(tc) + seg(tl)
    return pl.pallas_call(
        _mlstm_kernel,
        grid=(bsz, H_B // 2),
        in_specs=in_specs,
        out_specs=[col(tc, 0, 2 * LANES), col(tl, 0, 2 * LANES)],
        out_shape=[jax.ShapeDtypeStruct((bsz, tc, H_B * DV_B), F32),
                   jax.ShapeDtypeStruct((bsz, tl, H_B * DV_B), F32)],
        scratch_shapes=[pltpu.VMEM((tc, 2 * LANES), F32), pltpu.VMEM((tl, 2 * LANES), F32)],
        compiler_params=_params(("arbitrary", "arbitrary")),
        name="mlstm_scan",
    )(conv_w, conv_w, main_c, main_c, main_c, gate_c, grow_c, main_l, main_l, main_l, gate_l, grow_l)


def _mix_out_kernel(oa_ref, ob_ref, ag_ref, bo_ref, x_ref, mod_ref, og_ref, w_ref, ng_ref, xo_ref):
    og = og_ref[...]
    pieces = []
    for h in range(H_A):
        sl = slice(h * LANES, (h + 1) * LANES)
        a = ag_ref[:, sl]
        pieces.append((_rms(oa_ref[:, sl]) * og[:, sl] * (a * _sigmoid(a))).astype(BF16))
    for h in range(H_B):
        sl = slice(h * LANES, (h + 1) * LANES)
        gsl = slice((H_A + h) * LANES, (H_A + h + 1) * LANES)
        pieces.append((_rms(ob_ref[:, sl]) * og[:, gsl] * _sigmoid(bo_ref[:, sl])).astype(BF16))
    y = _dot(jnp.concatenate(pieces, axis=1), w_ref[...])
    xo_ref[...] = x_ref[...] + mod_ref[0][2:3] * (_rms(y) * ng_ref[...])


def _ffn_kernel(x_ref, mod_ref, g2_ref, g3_ref, win_ref, wout_ref, xo_ref, *, d_ff, chunk):
    x = x_ref[...]
    mod = mod_ref[0]
    h = _modulated(x, g2_ref[...], mod, 3).astype(BF16)
    acc = None
    for j in range(d_ff // chunk):
        g = _dot(h, win_ref[:, j * chunk:(j + 1) * chunk])
        u = _dot(h, win_ref[:, d_ff + j * chunk:d_ff + (j + 1) * chunk])
        a = (g * _sigmoid(g) * u).astype(BF16)
        part = _dot(a, wout_ref[j * chunk:(j + 1) * chunk, :])
        acc = part if acc is None else acc + part
    xo_ref[...] = x + mod[5:6] * (_rms(acc) * g3_ref[...])


def _ffn(x2d, tiles_per_mod, mod, g2, g3, w_in, w_out):
    d_ff = w_out.shape[0]
    chunk = d_ff // 2 if (d_ff // 2) % LANES == 0 else d_ff
    (out,) = _row_call(functools.partial(_ffn_kernel, d_ff=d_ff, chunk=chunk), x2d.shape[0], tiles_per_mod,
                       [(x2d, x2d.shape[1], 0)], mod, [g2, g3, w_in, w_out],
                       [(x2d.shape[1], F32)], "swiglu")
    return out


def _rope_rot(t):
    lane = lax.broadcasted_iota(jnp.int32, t.shape, 1)
    first = (lane & (HEAD_DIM // 2 - 1)) < HEAD_DIM // 4
    return jnp.where(first, pltpu.roll(t, LANES - HEAD_DIM // 4, 1), pltpu.roll(t, HEAD_DIM // 4, 1))


def _qkv_kernel(*refs, use_rope):
    if use_rope:
        x_ref, cos_ref, sin_ref, mod_ref, g_ref, w_ref, qkg_ref, q_ref, k_ref, v_ref = refs
        cos, sin = cos_ref[...], sin_ref[...]
    else:
        x_ref, mod_ref, g_ref, w_ref, qkg_ref, q_ref, k_ref, v_ref = refs
    h = _modulated(x_ref[...], g_ref[...], mod_ref[0], 0).astype(BF16)
    qkv = _dot(h, w_ref[...])
    qkg = qkg_ref[...]

    def head(j, g):
        t = _rms(qkv[:, j * LANES:(j + 1) * LANES]) * g
        if use_rope:
            t = t * cos + _rope_rot(t) * sin
        return t

    for j in range(N_HEADS_C):
        q_ref[:, j * LANES:(j + 1) * LANES] = (head(j, qkg[0:1]) * (HEAD_DIM ** -0.5)).astype(BF16)
    for j in range(N_KV_C):
        k_ref[:, j * LANES:(j + 1) * LANES] = head(N_HEADS_C + j, qkg[1:2]).astype(BF16)
    v0 = (N_HEADS_C + N_KV_C) * LANES
    v_ref[...] = qkv[:, v0:v0 + N_KV_C * LANES].astype(BF16)


def _rope_tables(t):
    axis_dim = HEAD_DIM // 2
    rows = t // GRID_W
    r, col = jnp.meshgrid(jnp.arange(rows, dtype=F32), jnp.arange(GRID_W, dtype=F32), indexing='ij')
    inv = jnp.power(ROPE_THETA, -jnp.arange(0, axis_dim, 2, dtype=F32) / axis_dim)
    ar = r.reshape(-1)[:, None] * inv
    ac = col.reshape(-1)[:, None] * inv
    cos = jnp.concatenate([jnp.cos(ar), jnp.cos(ar), jnp.cos(ac), jnp.cos(ac)], axis=1)
    sin = jnp.concatenate([-jnp.sin(ar), jnp.sin(ar), -jnp.sin(ac), jnp.sin(ac)], axis=1)
    return cos, sin


def _attn_kernel(q_ref, k_ref, vt_ref, o_ref):
    tq = q_ref.shape[1]
    q = jnp.concatenate([q_ref[0, :, g * LANES:(g + 1) * LANES] for g in range(GROUP_C)], axis=0)
    st = _dot_nt(k_ref[0], q)
    m = jnp.max(st, axis=0, keepdims=True)
    p = jnp.exp(st - m)
    l = jnp.sum(p, axis=0, keepdims=True)
    ot = _dot(vt_ref[0], p.astype(BF16)) / l
    o = ot.T
    for g in range(GROUP_C):
        o_ref[0, :, g * LANES:(g + 1) * LANES] = o[g * tq:(g + 1) * tq].astype(o_ref.dtype)


def _attention(q, k, vt):
    bsz, tq_all, _ = q.shape
    tk_all = k.shape[1]
    gw = GROUP_C * HEAD_DIM
    return pl.pallas_call(
        _attn_kernel,
        grid=(bsz, N_KV_C, tq_all // ATTN_TQ),
        in_specs=[pl.BlockSpec((1, ATTN_TQ, gw), lambda b, h, i: (b, i, h)),
                  pl.BlockSpec((1, tk_all, HEAD_DIM), lambda b, h, i: (b, 0, h)),
                  pl.BlockSpec((1, HEAD_DIM, tk_all), lambda b, h, i: (b, h, 0))],
        out_specs=pl.BlockSpec((1, ATTN_TQ, gw), lambda b, h, i: (b, i, h)),
        out_shape=jax.ShapeDtypeStruct(q.shape, BF16),
        compiler_params=_params(("arbitrary", "arbitrary", "arbitrary")),
        name="attention",
    )(q, k, vt)


def _attn_out_kernel(o_ref, x_ref, mod_ref, w_ref, ng_ref, xo_ref):
    y = _dot(o_ref[...], w_ref[...])
    xo_ref[...] = x_ref[...] + mod_ref[0][2:3] * (_rms(y) * ng_ref[...])


def kernel(x, c, ctx, c_ctx, ada_w, ada_b, norm_g, ffn_w_in, ffn_w_out, ab_w_in, ab_conv, ab_gate_b,
           hgrn_lb, ab_out_g, ab_w_out, attn_w_qkv, attn_qk_g, attn_w_out):
    bsz, tl, d = x.shape
    tc = ctx.shape[1]
    depth = ada_w.shape[0]
    rl, rc = bsz * tl, bsz * tc
    tiles_l, tiles_c = tl // ROW_TILE, rc // ROW_TILE

    cvecs = jnp.zeros((8, d), F32).at[:bsz].set(c).at[bsz].set(c_ctx)
    ada = _ada(cvecs, ada_w, ada_b)
    xl = x.reshape(rl, d)
    xc = ctx.reshape(rc, d)

    for l in range(depth):
        need_ctx = l < depth - 1
        mod_l = ada[l, :bsz].reshape(bsz, 6, d)
        mod_c = ada[l, bsz:bsz + 1].reshape(1, 6, d)
        ng = [norm_g[l, j][None] for j in range(4)]
        segs = [(xl, tl, tiles_l, mod_l), (xc, tc, tiles_c, mod_c)]
        if l % 2 == 0:
            e = l // 2
            w = ab_w_in[e]
            w_main = w[:, :AB_MAIN].astype(BF16)
            n_gate = w.shape[1] - AB_MAIN
            w_gate = jnp.zeros((d, LANES), F32).at[:, :n_gate].set(w[:, AB_MAIN:]).astype(BF16)
            gate_b = jnp.zeros((1, LANES), F32).at[0, :n_gate].set(ab_gate_b[e].reshape(-1))
            mains, gates = [], []
            for x2d, t, tpm, mod in segs:
                main, gate = _row_call(_proj_ab_kernel, x2d.shape[0], tpm, [(x2d, d, 0)], mod,
                                       [ng[0], w_main, w_gate, gate_b],
                                       [(AB_MAIN, F32), (LANES, F32)], "proj_ab")
                mains.append(main.reshape(bsz, t, AB_MAIN))
                gates.append(gate.reshape(bsz, t, LANES))
            oa_c, oa_l = _hgrn(mains[1], mains[0], hgrn_lb, l)
            ob_c, ob_l = _mlstm(mains[1], mains[0], gates[1], gates[0], ab_conv[e])
            w_out = ab_w_out[e].astype(BF16)
            og = ab_out_g[e][None]
            new = []
            for (x2d, t, tpm, mod), main, oa, ob in zip(segs, mains, (oa_l, oa_c), (ob_l, ob_c)):
                if x2d is xc and not need_ctx:
                    new.append(x2d)
                    continue
                m2d = main.reshape(-1, AB_MAIN)
                (xo,) = _row_call(_mix_out_kernel, x2d.shape[0], tpm,
                                  [(oa.reshape(-1, H_A * DV_A), H_A * DV_A, 0),
                                   (ob.reshape(-1, H_B * DV_B), H_B * DV_B, 0),
                                   (m2d, H_A * DV_A, _COL_AG // H_A), (m2d, H_B * DV_B, _COL_BO // H_B),
                                   (x2d, d, 0)], mod, [og, w_out, ng[1]], [(d, F32)], "mix_out")
                new.append(xo)
            xl, xc = new
        else:
            o = l // 2
            w_qkv = attn_w_qkv[o].astype(BF16)
            qkg = attn_qk_g[o]
            cos, sin = _rope_tables(tl)
            cos = jnp.tile(cos, (bsz, 1))
            sin = jnp.tile(sin, (bsz, 1))
            nq, nkv = N_HEADS_C * HEAD_DIM, N_KV_C * HEAD_DIM
            outs = [(nq, BF16), (nkv, BF16), (nkv, BF16)]
            q_l, k_l, v_l = _row_call(functools.partial(_qkv_kernel, use_rope=True), rl, tiles_l,
                                      [(xl, d, 0), (cos, LANES, 0), (sin, LANES, 0)], mod_l,
                                      [ng[0], w_qkv, qkg], outs, "qkv_lat")
            q_c, k_c, v_c = _row_call(functools.partial(_qkv_kernel, use_rope=False), rc, tiles_c,
                                      [(xc, d, 0)], mod_c, [ng[0], w_qkv, qkg], outs, "qkv_ctx")
            k_all = jnp.concatenate([k_l.reshape(bsz, tl, nkv), k_c.reshape(bsz, tc, nkv)], axis=1)
            v_all = jnp.concatenate([v_l.reshape(bsz, tl, nkv), v_c.reshape(bsz, tc, nkv)], axis=1)
            w_out = attn_w_out[o].astype(BF16)
            att = _attention(q_l.reshape(bsz, tl, nq), k_all, jnp.swapaxes(v_all, 1, 2))
            (xl_new,) = _row_call(_attn_out_kernel, rl, tiles_l, [(att.reshape(rl, nq), nq, 0), (xl, d, 0)],
                                  mod_l, [w_out, ng[1]], [(d, F32)], "attn_out")
            if need_ctx:
                att_c = _attention(q_c.reshape(bsz, tc, nq), k_c.reshape(bsz, tc, nkv),
                                   jnp.swapaxes(v_c.reshape(bsz, tc, nkv), 1, 2))
                (xc,) = _row_call(_attn_out_kernel, rc, tiles_c, [(att_c.reshape(rc, nq), nq, 0), (xc, d, 0)],
                                  mod_c, [w_out, ng[1]], [(d, F32)], "attn_out_ctx")
            xl = xl_new
        w_in = ffn_w_in[l].astype(BF16)
        w_o = ffn_w_out[l].astype(BF16)
        xl = _ffn(xl, tiles_l, mod_l, ng[2], ng[3], w_in, w_o)
        if need_ctx:
            xc = _ffn(xc, tiles_c, mod_c, ng[2], ng[3], w_in, w_o)
    return xl.reshape(bsz, tl, d)
```

```python
import functools

import jax
import jax.numpy as jnp
from jax import lax
from jax.experimental import pallas as pl
from jax.experimental.pallas import tpu as pltpu

F32 = jnp.float32
BF16 = jnp.bfloat16
EPS = 1e-6

H_A, DK_A, DV_A = 4, 128, 128
H_B, DK_B, DV_B = 4, 64, 128
N_HEADS_C, N_KV_C, HEAD_DIM = 8, 2, 128
GROUP_C = N_HEADS_C // N_KV_C
GRID_W = 64
ROPE_THETA = 10000.0
HGRN_CHUNK = 32
Q_SCALE = HEAD_DIM ** -0.5 * 1.4426950408889634

LANES = 128
SCAN_BLOCK = 128
SCAN_UNROLL = 2
ROW_TILE = 512
ATTN_TQ = 256
VMEM_LIMIT = 56 * 1024 * 1024

_COL_AQ, _COL_AI, _COL_AG, _COL_AFF, _COL_AFB = 0, 4, 8, 12, 16
_COL_BQ, _COL_BK, _COL_BV, _COL_BO = 20, 22, 24, 28
AB_MAIN = 32 * LANES


def _dot(a, b):
    return jnp.dot(a, b, preferred_element_type=F32)


def _dot_nt(a, b):
    return lax.dot_general(a, b, (((1,), (1,)), ((), ())), preferred_element_type=F32)


def _dot_tn(a, b):
    return lax.dot_general(a, b, (((0,), (0,)), ((), ())), preferred_element_type=F32)


def _sigmoid(x):
    return 1.0 / (1.0 + jnp.exp(-x))


def _log_sigmoid(x):
    return jnp.minimum(x, 0.0) - jnp.log(1.0 + jnp.exp(-jnp.abs(x)))


def _rms(x):
    return x * lax.rsqrt(jnp.mean(x * x, axis=-1, keepdims=True) + EPS)


def _modulated(x, g, mod, si):
    return _rms(x) * g * (1.0 + mod[si + 1:si + 2]) + mod[si:si + 1]


def _split3(x):
    hi = x.astype(BF16)
    r = x - hi.astype(F32)
    mid = r.astype(BF16)
    lo = (r - mid.astype(F32)).astype(BF16)
    return hi, mid, lo


def _params(sem):
    return pltpu.CompilerParams(dimension_semantics=sem, vmem_limit_bytes=VMEM_LIMIT)


def _ada_kernel(c_ref, w_ref, b_ref, o_ref):
    c = c_ref[...]
    s = c * _sigmoid(c)
    o_ref[0] = jnp.dot(s, w_ref[0], preferred_element_type=F32,
                       precision=lax.Precision.HIGHEST) + b_ref[0]


def _ada(cvecs, ada_w, ada_b):
    depth, d, n = ada_w.shape
    tn = 1024
    return pl.pallas_call(
        _ada_kernel,
        grid=(depth, n // tn),
        in_specs=[pl.BlockSpec((8, d), lambda l, j: (0, 0)),
                  pl.BlockSpec((1, d, tn), lambda l, j: (l, 0, j)),
                  pl.BlockSpec((1, 1, tn), lambda l, j: (l, 0, j))],
        out_specs=pl.BlockSpec((1, 8, tn), lambda l, j: (l, 0, j)),
        out_shape=jax.ShapeDtypeStruct((depth, 8, n), F32),
        compiler_params=_params(("arbitrary", "arbitrary")),
        name="ada",
    )(cvecs, ada_w, ada_b.reshape(depth, 1, n))


def _row_call(kernel, rows, tiles_per_mod, row_ins, mod, consts, outs, name):
    nt = rows // ROW_TILE
    in_specs = []
    args = []
    for arr, width, cb in row_ins:
        in_specs.append(pl.BlockSpec((ROW_TILE, width), lambda i, cb=cb: (i, cb)))
        args.append(arr)
    in_specs.append(pl.BlockSpec((1,) + mod.shape[1:], lambda i: (i // tiles_per_mod, 0, 0)))
    args.append(mod)
    for cst in consts:
        in_specs.append(pl.BlockSpec(cst.shape, lambda i, nd=cst.ndim: (0,) * nd))
        args.append(cst)
    out_specs = [pl.BlockSpec((ROW_TILE, w), lambda i: (i, 0)) for w, _ in outs]
    out_shape = [jax.ShapeDtypeStruct((rows, w), dt) for w, dt in outs]
    return pl.pallas_call(
        kernel, grid=(nt,), in_specs=in_specs, out_specs=out_specs, out_shape=out_shape,
        compiler_params=_params(("arbitrary",)), name=name)(*args)


def _proj_ab_kernel(x_ref, mod_ref, g_ref, w_ref, wg_ref, gb_ref, main_ref, gate_ref):
    h = _modulated(x_ref[...], g_ref[...], mod_ref[0], 0).astype(BF16)
    main_ref[...] = _dot(h, w_ref[...])
    gate_ref[...] = _dot(h, wg_ref[...]) + gb_ref[...]


def _hgrn_blocks(blocks, states, lbs, tris, tri3s):
    nch = SCAN_BLOCK // HGRN_CHUNK
    half = HGRN_CHUNK // 2
    nblk = len(blocks)
    ks, cats = [], []
    for q, v, z, rev in blocks:
        lb = lbs[rev]
        one_m_lb = 1.0 - lb
        logf = jnp.log(lb + one_m_lb * _sigmoid(z))
        ks.append(one_m_lb * _sigmoid(-z))
        cats.append(jnp.concatenate(_split3(logf), axis=0))
    bs = [_dot(tri3s[blocks[i][3]], cats[i]) for i in range(nblk)]
    scaled = []
    for i, (q, v, z, rev) in enumerate(blocks):
        b, k = bs[i], ks[i]
        bmid, blast, blast_rows = [], [], []
        for c in range(nch):
            bc = b[c * HGRN_CHUNK:(c + 1) * HGRN_CHUNK]
            r_mid = half if rev else half - 1
            r_last = 0 if rev else HGRN_CHUNK - 1
            bmid.append(jnp.broadcast_to(bc[r_mid:r_mid + 1], bc.shape))
            blast_rows.append(bc[r_last:r_last + 1])
            blast.append(jnp.broadcast_to(bc[r_last:r_last + 1], bc.shape))
        bmid = jnp.concatenate(bmid, axis=0)
        blast = jnp.concatenate(blast, axis=0)
        e1 = b - bmid
        qt = (q * jnp.exp(e1)).astype(BF16)
        kt = (k * jnp.exp(-e1)).astype(BF16)
        qh = (q * jnp.exp(b)).astype(BF16)
        kh = (k * jnp.exp(blast - b)).astype(BF16)
        scaled.append((qt, kt, qh, kh, v.astype(BF16), [jnp.exp(r) for r in blast_rows]))
    row = lax.broadcasted_iota(jnp.int32, (SCAN_BLOCK, DK_A), 0)
    in_chunk = [(row >= c * HGRN_CHUNK) & (row < (c + 1) * HGRN_CHUNK) for c in range(nch)]
    zero = jnp.zeros((SCAN_BLOCK, DK_A), BF16)

    def slabs(x):
        return jnp.concatenate([jnp.where(in_chunk[c], x, zero) for c in range(nch)], axis=1)

    araw = [_dot_nt(s[0], s[1]) for s in scaled]
    incrs = [_dot_tn(s[4], slabs(s[3])) for s in scaled]
    outs = [_dot(jnp.where(tris[blocks[i][3]], araw[i], 0.0).astype(BF16), scaled[i][4]) for i in range(nblk)]
    states = list(states)
    befores = []
    for i, (q, v, z, rev) in enumerate(blocks):
        st, incr, decay = states[rev], incrs[i], scaled[i][5]
        before = [None] * nch
        for c in (range(nch - 1, -1, -1) if rev else range(nch)):
            before[c] = st.astype(BF16)
            st = st * decay[c] + incr[:, c * DK_A:(c + 1) * DK_A]
        states[rev] = st
        befores.append(jnp.concatenate(before, axis=1))
    outs = [outs[i] + _dot_nt(slabs(scaled[i][2]), befores[i]) for i in range(nblk)]
    return outs, tuple(states)


def _chunk_tri(rev):
    r = lax.broadcasted_iota(jnp.int32, (SCAN_BLOCK, SCAN_BLOCK), 0)
    c = lax.broadcasted_iota(jnp.int32, (SCAN_BLOCK, SCAN_BLOCK), 1)
    shift = HGRN_CHUNK.bit_length() - 1
    same = lax.shift_right_logical(r, shift) == lax.shift_right_logical(c, shift)
    return same & ((c >= r) if rev else (c <= r))


def _tri3(tri):
    t = jnp.where(tri, 1.0, 0.0).astype(BF16)
    return jnp.concatenate([t, t, t], axis=1)


def _hgrn_kernel(lb_ref, qc, vc, zfc, zbc, ql, vl, zfl, zbl, oc_ref, ol_ref, sc_c, sc_l, *, layer):
    raw = lb_ref[0]
    lbs = []
    for d in range(2):
        x = raw[d]
        e = jnp.exp(x - jnp.max(x, axis=0, keepdims=True))
        lbs.append(jnp.sum(e[:layer + 1], axis=0, keepdims=True) / jnp.sum(e, axis=0, keepdims=True))
    tris = [_chunk_tri(False), _chunk_tri(True)]
    tri3s = [_tri3(t) for t in tris]

    def run(q, v, zf, zb, o_ref, scr, carry):
        nb = q.shape[1] // SCAN_BLOCK
        assert nb % SCAN_UNROLL == 0

        def body(i, carry):
            rows, blocks = [], []
            for u in range(SCAN_UNROLL):
                jf = i * SCAN_UNROLL + u
                for rev, z, j in ((False, zf, jf), (True, zb, nb - 1 - jf)):
                    r = pl.ds(pl.multiple_of(j * SCAN_BLOCK, SCAN_BLOCK), SCAN_BLOCK)
                    rows.append(r)
                    blocks.append((q[0, r, :], v[0, r, :], z[0, r, :], rev))
            outs, carry = _hgrn_blocks(blocks, carry, lbs, tris, tri3s)
            for r, o, blk in zip(rows, outs, blocks):
                if blk[3]:
                    scr[r, :] = o
                else:
                    o_ref[0, r, :] = o
            return carry

        carry = lax.fori_loop(0, nb // SCAN_UNROLL, body, carry)

        def add(i, _):
            r = pl.ds(pl.multiple_of(i * SCAN_BLOCK, SCAN_BLOCK), SCAN_BLOCK)
            o_ref[0, r, :] = o_ref[0, r, :] + scr[r, :]
            return 0

        lax.fori_loop(0, nb, add, 0)
        return carry

    zero = jnp.zeros((DV_A, DK_A), F32)
    carry = run(qc, vc, zfc, zbc, oc_ref, sc_c, (zero, zero))
    run(ql, vl, zfl, zbl, ol_ref, sc_l, carry)


def _hgrn(main_c, main_l, hgrn_lb, layer):
    bsz, tc, _ = main_c.shape
    tl = main_l.shape[1]
    nl = hgrn_lb.shape[1]
    lb = hgrn_lb.reshape(2, nl, H_A, DK_A).transpose(2, 0, 1, 3)

    def col(t, cb):
        return pl.BlockSpec((1, t, LANES), lambda b, h, cb=cb: (b, 0, cb + h))

    in_specs = [pl.BlockSpec((1, 2, nl, DK_A), lambda b, h: (h, 0, 0, 0))]
    in_specs += [col(tc, c) for c in (_COL_AQ, _COL_AI, _COL_AFF, _COL_AFB)]
    in_specs += [col(tl, c) for c in (_COL_AQ, _COL_AI, _COL_AFF, _COL_AFB)]
    return pl.pallas_call(
        functools.partial(_hgrn_kernel, layer=layer),
        grid=(bsz, H_A),
        in_specs=in_specs,
        out_specs=[col(tc, 0), col(tl, 0)],
        out_shape=[jax.ShapeDtypeStruct((bsz, tc, H_A * DV_A), F32),
                   jax.ShapeDtypeStruct((bsz, tl, H_A * DV_A), F32)],
        scratch_shapes=[pltpu.VMEM((tc, LANES), F32), pltpu.VMEM((tl, LANES), F32)],
        compiler_params=_params(("arbitrary", "arbitrary")),
        name="hgrn2_scan",
    )(lb, main_c, main_c, main_c, main_c, main_l, main_l, main_l, main_l)


def _conv_silu(ref, i, nb, w):
    t = nb * SCAN_BLOCK
    r0 = pl.multiple_of(i * SCAN_BLOCK, SCAN_BLOCK)
    x = ref[0, pl.ds(r0, SCAN_BLOCK), :]
    p0 = pl.multiple_of(jnp.maximum(r0 - 8, 0), 8)
    n0 = pl.multiple_of(jnp.minimum(r0 + SCAN_BLOCK, t - 8), 8)
    prev = ref[0, pl.ds(p0, 8), :][7:8] * jnp.where(i > 0, 1.0, 0.0)
    nxt = ref[0, pl.ds(n0, 8), :][0:1] * jnp.where(i < nb - 1, 1.0, 0.0)
    row = lax.broadcasted_iota(jnp.int32, x.shape, 0)
    xp = jnp.where(row == 0, prev, pltpu.roll(x, 1, 0))
    xn = jnp.where(row == SCAN_BLOCK - 1, nxt, pltpu.roll(x, SCAN_BLOCK - 1, 0))
    y = w[0:1] * xp + w[1:2] * x + w[2:3] * xn
    return y * _sigmoid(y)


def _mlstm_blocks(blocks, states, cst):
    low, upp, low3v, upp3v, head_lanes = cst
    nblk = len(blocks)
    cats = [jnp.concatenate(_split3(_log_sigmoid(blk[3])), axis=1) for blk in blocks]
    brows = [_dot(cats[i], low3v if blocks[i][4] else upp3v) for i in range(nblk)]
    pad = jnp.zeros((SCAN_BLOCK - 16, SCAN_BLOCK), F32)
    cols = []
    for i, blk in enumerate(blocks):
        e = blk[3] - pltpu.roll(brows[i], 16 - 2, 0)
        cols.append(jnp.concatenate([e, pad], axis=0).T)
    states = [list(s) for s in states]
    chains = []
    for i, (q2, k2, v2, grow, rev) in enumerate(blocks):
        last = 0 if rev else SCAN_BLOCK - 1
        for hd in range(2):
            ct, n, m = states[rev][hd]
            brow = brows[i][4 * rev + 2 + hd:4 * rev + 3 + hd]
            cvec = cols[i][:, 4 * rev + hd:4 * rev + hd + 1]
            blast = brow[:, last:last + 1]
            g = blast + cvec
            m_new = jnp.maximum(blast + m, jnp.max(g, axis=0, keepdims=True))
            wk = jnp.exp(g - m_new)
            decay = jnp.exp(blast + m - m_new)
            chains.append(dict(i=i, hd=hd, rev=rev, brow=brow, cvec=cvec, m=m, wk=wk, decay=decay))
            states[rev][hd] = (ct, n, m_new)
    for ch in chains:
        q2, k2, v2, grow, rev = blocks[ch["i"]]
        ch["qb"] = jnp.where(head_lanes[ch["hd"]], q2, 0.0).astype(BF16)
        ch["vb"] = v2[:, ch["hd"] * DV_B:(ch["hd"] + 1) * DV_B].astype(BF16)
        ch["sraw"] = _dot_nt(k2.astype(BF16), ch["qb"])
    for ch in chains:
        k2, rev = blocks[ch["i"]][1], ch["rev"]
        d = jnp.where(low if rev else upp, ch["brow"] + ch["cvec"], -jnp.inf)
        m_inter = ch["brow"] + ch["m"]
        m_t = jnp.maximum(jnp.max(d, axis=0, keepdims=True), m_inter)
        s = ch["sraw"] * jnp.exp(d - m_t)
        kw = k2 * ch["wk"]
        ch.update(m_t=m_t, w_inter=jnp.exp(m_inter - m_t), sb=s.astype(BF16), kwb=kw.astype(BF16),
                  ssum=jnp.sum(s, axis=0, keepdims=True), ksum=jnp.sum(kw, axis=0, keepdims=True))
    for ch in chains:
        ch["num"] = _dot_tn(ch["vb"], ch["sb"])
        ch["incr"] = _dot_tn(ch["vb"], ch["kwb"])
    for ch in chains:
        ct, n, m_new = states[ch["rev"]][ch["hd"]]
        ch["aug"] = jnp.concatenate([ct, jnp.broadcast_to(n, (16, LANES))], axis=0).astype(BF16)
        states[ch["rev"]][ch["hd"]] = (ch["decay"] * ct + ch["incr"], ch["decay"] * n + ch["ksum"], m_new)
    outs = [[None, None] for _ in range(nblk)]
    for ch in chains:
        inter = _dot_nt(ch["aug"], ch["qb"])
        num = ch["num"] + ch["w_inter"] * inter[:DV_B]
        den = ch["ssum"] + ch["w_inter"] * inter[DV_B:DV_B + 1]
        ht = num / jnp.maximum(jnp.abs(den), jnp.exp(-ch["m_t"]))
        outs[ch["i"]][ch["hd"]] = ht.T
    return [jnp.concatenate(o, axis=1) for o in outs], tuple(tuple(s) for s in states)


def _mlstm_kernel(wq_ref, wk_ref, qc, kc, vc, grc, ql, kl, vl, grl, oc_ref, ol_ref, sc_c, sc_l):
    wq, wk = wq_ref[...], wk_ref[...]
    r = lax.broadcasted_iota(jnp.int32, (SCAN_BLOCK, SCAN_BLOCK), 0)
    c = lax.broadcasted_iota(jnp.int32, (SCAN_BLOCK, SCAN_BLOCK), 1)
    low, upp = c <= r, c >= r
    low_b = jnp.where(low, 1.0, 0.0).astype(BF16)
    upp_b = jnp.where(upp, 1.0, 0.0).astype(BF16)
    lane = lax.broadcasted_iota(jnp.int32, (SCAN_BLOCK, LANES), 1)
    cst = (low, upp, jnp.concatenate([low_b] * 3, axis=0), jnp.concatenate([upp_b] * 3, axis=0),
           [lane < DK_B, lane >= DK_B])

    def run(qr, kr, vr, grr, o_ref, scr, carry):
        nb = qr.shape[1] // SCAN_BLOCK
        assert nb % SCAN_UNROLL == 0

        def body(i, carry):
            rows, blocks = [], []
            for u in range(SCAN_UNROLL):
                jf = i * SCAN_UNROLL + u
                for rev, j in ((False, jf), (True, nb - 1 - jf)):
                    rr = pl.ds(pl.multiple_of(j * SCAN_BLOCK, SCAN_BLOCK), SCAN_BLOCK)
                    rows.append(rr)
                    blocks.append((_conv_silu(qr, j, nb, wq) * (DK_B ** -0.5), _conv_silu(kr, j, nb, wk),
                                   vr[0, rr, :], grr[0, 0, :, rr], rev))
            outs, carry = _mlstm_blocks(blocks, carry, cst)
            for rr, o, blk in zip(rows, outs, blocks):
                if blk[4]:
                    scr[rr, :] = o
                else:
                    o_ref[0, rr, :] = o
            return carry

        carry = lax.fori_loop(0, nb // SCAN_UNROLL, body, carry)

        def add(i, _):
            rr = pl.ds(pl.multiple_of(i * SCAN_BLOCK, SCAN_BLOCK), SCAN_BLOCK)
            o_ref[0, rr, :] = o_ref[0, rr, :] + scr[rr, :]
            return 0

        lax.fori_loop(0, nb, add, 0)
        return carry

    s0 = (jnp.zeros((DV_B, LANES), F32), jnp.zeros((1, LANES), F32), jnp.zeros((1, 1), F32))
    carry = run(qc, kc, vc, grc, oc_ref, sc_c, ((s0, s0), (s0, s0)))
    run(ql, kl, vl, grl, ol_ref, sc_l, carry)


def _gate_rows(gate):
    bsz, t, _ = gate.shape
    g = gate[:, :, :4 * H_B].reshape(bsz, t, 4, H_B // 2, 2)
    g = g.transpose(0, 3, 2, 4, 1).reshape(bsz, H_B // 2, 8, t)
    return jnp.concatenate([g, jnp.zeros_like(g)], axis=2)


def _mlstm(main_c, main_l, gate_c, gate_l, conv_w):
    bsz, tc, _ = main_c.shape
    tl = main_l.shape[1]

    def col(t, cb, w=LANES):
        return pl.BlockSpec((1, t, w), lambda b, p, cb=cb: (b, 0, cb + p))

    def seg(t):
        return [col(t, _COL_BQ), col(t, _COL_BK), col(t, _COL_BV // 2, 2 * LANES),
                pl.BlockSpec((1, 1, 16, t), lambda b, p: (b, p, 0, 0))]

    in_specs = [pl.BlockSpec((3, LANES), lambda b, p: (0, p)),
                pl.BlockSpec((3, LANES), lambda b, p: (0, 2 + p))] + seg(tc) + seg(tl)
    return pl.pallas_call(
        _mlstm_kernel,
        grid=(bsz, H_B // 2),
        in_specs=in_specs,
        out_specs=[col(tc, 0, 2 * LANES), col(tl, 0, 2 * LANES)],
        out_shape=[jax.ShapeDtypeStruct((bsz, tc, H_B * DV_B), F32),
                   jax.ShapeDtypeStruct((bsz, tl, H_B * DV_B), F32)],
        scratch_shapes=[pltpu.VMEM((tc, 2 * LANES), F32), pltpu.VMEM((tl, 2 * LANES), F32)],
        compiler_params=_params(("arbitrary", "arbitrary")),
        name="mlstm_scan",
    )(conv_w, conv_w, main_c, main_c, main_c, _gate_rows(gate_c), main_l, main_l, main_l, _gate_rows(gate_l))


def _mix_out_kernel(oa_ref, ob_ref, ag_ref, bo_ref, x_ref, mod_ref, og_ref, w_ref, ng_ref, xo_ref):
    og = og_ref[...]
    pieces = []
    for h in range(H_A):
        sl = slice(h * LANES, (h + 1) * LANES)
        a = ag_ref[:, sl]
        pieces.append((_rms(oa_ref[:, sl]) * og[:, sl] * (a * _sigmoid(a))).astype(BF16))
    for h in range(H_B):
        sl = slice(h * LANES, (h + 1) * LANES)
        gsl = slice((H_A + h) * LANES, (H_A + h + 1) * LANES)
        pieces.append((_rms(ob_ref[:, sl]) * og[:, gsl] * _sigmoid(bo_ref[:, sl])).astype(BF16))
    y = _dot(jnp.concatenate(pieces, axis=1), w_ref[...])
    xo_ref[...] = x_ref[...] + mod_ref[0][2:3] * (_rms(y) * ng_ref[...])


def _ffn_kernel(x_ref, mod_ref, g2_ref, g3_ref, win_ref, wout_ref, xo_ref, *, d_ff, chunk):
    x = x_ref[...]
    mod = mod_ref[0]
    h = _modulated(x, g2_ref[...], mod, 3).astype(BF16)
    acc = None
    for j in range(d_ff // chunk):
        g = _dot(h, win_ref[:, j * chunk:(j + 1) * chunk])
        u = _dot(h, win_ref[:, d_ff + j * chunk:d_ff + (j + 1) * chunk])
        a = (g * _sigmoid(g) * u).astype(BF16)
        part = _dot(a, wout_ref[j * chunk:(j + 1) * chunk, :])
        acc = part if acc is None else acc + part
    xo_ref[...] = x + mod[5:6] * (_rms(acc) * g3_ref[...])


def _ffn(x2d, tiles_per_mod, mod, g2, g3, w_in, w_out):
    d_ff = w_out.shape[0]
    chunk = d_ff // 2 if (d_ff // 2) % LANES == 0 else d_ff
    (out,) = _row_call(functools.partial(_ffn_kernel, d_ff=d_ff, chunk=chunk), x2d.shape[0], tiles_per_mod,
                       [(x2d, x2d.shape[1], 0)], mod, [g2, g3, w_in, w_out],
                       [(x2d.shape[1], F32)], "swiglu")
    return out


def _rope_rot(t):
    lane = lax.broadcasted_iota(jnp.int32, t.shape, 1)
    first = (lane & (HEAD_DIM // 2 - 1)) < HEAD_DIM // 4
    return jnp.where(first, pltpu.roll(t, LANES - HEAD_DIM // 4, 1), pltpu.roll(t, HEAD_DIM // 4, 1))


def _qkv_kernel(*refs, use_rope):
    if use_rope:
        x_ref, cos_ref, sin_ref, mod_ref, g_ref, w_ref, qkg_ref, q_ref, k_ref, v_ref = refs
        cos, sin = cos_ref[...], sin_ref[...]
    else:
        x_ref, mod_ref, g_ref, w_ref, qkg_ref, q_ref, k_ref, v_ref = refs
    h = _modulated(x_ref[...], g_ref[...], mod_ref[0], 0).astype(BF16)
    qkv = _dot(h, w_ref[...])
    qkg = qkg_ref[...]

    def head(j, g):
        t = _rms(qkv[:, j * LANES:(j + 1) * LANES]) * g
        if use_rope:
            t = t * cos + _rope_rot(t) * sin
        return t

    for j in range(N_HEADS_C):
        q_ref[:, j * LANES:(j + 1) * LANES] = (head(j, qkg[0:1]) * Q_SCALE).astype(BF16)
    for j in range(N_KV_C):
        k_ref[:, j * LANES:(j + 1) * LANES] = head(N_HEADS_C + j, qkg[1:2]).astype(BF16)
    v0 = (N_HEADS_C + N_KV_C) * LANES
    v_ref[...] = qkv[:, v0:v0 + N_KV_C * LANES].astype(BF16)


def _rope_tables(t):
    axis_dim = HEAD_DIM // 2
    rows = t // GRID_W
    r, col = jnp.meshgrid(jnp.arange(rows, dtype=F32), jnp.arange(GRID_W, dtype=F32), indexing='ij')
    inv = jnp.power(ROPE_THETA, -jnp.arange(0, axis_dim, 2, dtype=F32) / axis_dim)
    ar = r.reshape(-1)[:, None] * inv
    ac = col.reshape(-1)[:, None] * inv
    cos = jnp.concatenate([jnp.cos(ar), jnp.cos(ar), jnp.cos(ac), jnp.cos(ac)], axis=1)
    sin = jnp.concatenate([-jnp.sin(ar), jnp.sin(ar), -jnp.sin(ac), jnp.sin(ac)], axis=1)
    return cos, sin


def _attn_kernel(q_ref, k_ref, vt_ref, o_ref):
    tq = q_ref.shape[1]
    q = jnp.concatenate([q_ref[0, :, g * LANES:(g + 1) * LANES] for g in range(GROUP_C)], axis=0)
    st = _dot_nt(k_ref[0], q)
    m = jnp.max(st, axis=0, keepdims=True)
    p = jnp.exp2(st - m)
    l = jnp.sum(p, axis=0, keepdims=True)
    o = (_dot(vt_ref[0], p.astype(BF16)) / l).T
    for g in range(GROUP_C):
        o_ref[0, :, g * LANES:(g + 1) * LANES] = o[g * tq:(g + 1) * tq].astype(o_ref.dtype)


def _attention(q, k, vt):
    bsz, tq_all, _ = q.shape
    tk_all = k.shape[1]
    gw = GROUP_C * HEAD_DIM
    return pl.pallas_call(
        _attn_kernel,
        grid=(bsz, N_KV_C, tq_all // ATTN_TQ),
        in_specs=[pl.BlockSpec((1, ATTN_TQ, gw), lambda b, h, i: (b, i, h)),
                  pl.BlockSpec((1, tk_all, HEAD_DIM), lambda b, h, i: (b, 0, h)),
                  pl.BlockSpec((1, HEAD_DIM, tk_all), lambda b, h, i: (b, h, 0))],
        out_specs=pl.BlockSpec((1, ATTN_TQ, gw), lambda b, h, i: (b, i, h)),
        out_shape=jax.ShapeDtypeStruct(q.shape, BF16),
        compiler_params=_params(("arbitrary", "arbitrary", "arbitrary")),
        name="attention",
    )(q, k, vt)


def _attn_out_kernel(o_ref, x_ref, mod_ref, w_ref, ng_ref, xo_ref):
    y = _dot(o_ref[...], w_ref[...])
    xo_ref[...] = x_ref[...] + mod_ref[0][2:3] * (_rms(y) * ng_ref[...])


def kernel(x, c, ctx, c_ctx, ada_w, ada_b, norm_g, ffn_w_in, ffn_w_out, ab_w_in, ab_conv, ab_gate_b,
           hgrn_lb, ab_out_g, ab_w_out, attn_w_qkv, attn_qk_g, attn_w_out):
    bsz, tl, d = x.shape
    tc = ctx.shape[1]
    depth = ada_w.shape[0]
    rl, rc = bsz * tl, bsz * tc
    tiles_l, tiles_c = tl // ROW_TILE, rc // ROW_TILE

    cvecs = jnp.zeros((8, d), F32).at[:bsz].set(c).at[bsz].set(c_ctx)
    ada = _ada(cvecs, ada_w, ada_b)
    xl = x.reshape(rl, d)
    xc = ctx.reshape(rc, d)

    for l in range(depth):
        need_ctx = l < depth - 1
        mod_l = ada[l, :bsz].reshape(bsz, 6, d)
        mod_c = ada[l, bsz:bsz + 1].reshape(1, 6, d)
        ng = [norm_g[l, j][None] for j in range(4)]
        segs = [(xl, tl, tiles_l, mod_l), (xc, tc, tiles_c, mod_c)]
        if l % 2 == 0:
            e = l // 2
            w = ab_w_in[e]
            w_main = w[:, :AB_MAIN].astype(BF16)
            n_gate = w.shape[1] - AB_MAIN
            w_gate = jnp.zeros((d, LANES), F32).at[:, :n_gate].set(w[:, AB_MAIN:]).astype(BF16)
            gate_b = jnp.zeros((1, LANES), F32).at[0, :n_gate].set(ab_gate_b[e].reshape(-1))
            mains, gates = [], []
            for x2d, t, tpm, mod in segs:
                main, gate = _row_call(_proj_ab_kernel, x2d.shape[0], tpm, [(x2d, d, 0)], mod,
                                       [ng[0], w_main, w_gate, gate_b],
                                       [(AB_MAIN, F32), (LANES, F32)], "proj_ab")
                mains.append(main.reshape(bsz, t, AB_MAIN))
                gates.append(gate.reshape(bsz, t, LANES))
            oa_c, oa_l = _hgrn(mains[1], mains[0], hgrn_lb, l)
            ob_c, ob_l = _mlstm(mains[1], mains[0], gates[1], gates[0], ab_conv[e])
            w_out = ab_w_out[e].astype(BF16)
            og = ab_out_g[e][None]
            new = []
            for (x2d, t, tpm, mod), main, oa, ob in zip(segs, mains, (oa_l, oa_c), (ob_l, ob_c)):
                if x2d is xc and not need_ctx:
                    new.append(x2d)
                    continue
                m2d = main.reshape(-1, AB_MAIN)
                (xo,) = _row_call(_mix_out_kernel, x2d.shape[0], tpm,
                                  [(oa.reshape(-1, H_A * DV_A), H_A * DV_A, 0),
                                   (ob.reshape(-1, H_B * DV_B), H_B * DV_B, 0),
                                   (m2d, H_A * DV_A, _COL_AG // H_A), (m2d, H_B * DV_B, _COL_BO // H_B),
                                   (x2d, d, 0)], mod, [og, w_out, ng[1]], [(d, F32)], "mix_out")
                new.append(xo)
            xl, xc = new
        else:
            o = l // 2
            w_qkv = attn_w_qkv[o].astype(BF16)
            qkg = attn_qk_g[o]
            cos, sin = _rope_tables(tl)
            cos = jnp.tile(cos, (bsz, 1))
            sin = jnp.tile(sin, (bsz, 1))
            nq, nkv = N_HEADS_C * HEAD_DIM, N_KV_C * HEAD_DIM
            outs = [(nq, BF16), (nkv, BF16), (nkv, BF16)]
            q_l, k_l, v_l = _row_call(functools.partial(_qkv_kernel, use_rope=True), rl, tiles_l,
                                      [(xl, d, 0), (cos, LANES, 0), (sin, LANES, 0)], mod_l,
                                      [ng[0], w_qkv, qkg], outs, "qkv_lat")
            q_c, k_c, v_c = _row_call(functools.partial(_qkv_kernel, use_rope=False), rc, tiles_c,
                                      [(xc, d, 0)], mod_c, [ng[0], w_qkv, qkg], outs, "qkv_ctx")
            k_all = jnp.concatenate([k_l.reshape(bsz, tl, nkv), k_c.reshape(bsz, tc, nkv)], axis=1)
            v_all = jnp.concatenate([v_l.reshape(bsz, tl, nkv), v_c.reshape(bsz, tc, nkv)], axis=1)
            w_out = attn_w_out[o].astype(BF16)
            att = _attention(q_l.reshape(bsz, tl, nq), k_all, jnp.swapaxes(v_all, 1, 2))
            (xl_new,) = _row_call(_attn_out_kernel, rl, tiles_l, [(att.reshape(rl, nq), nq, 0), (xl, d, 0)],
                                  mod_l, [w_out, ng[1]], [(d, F32)], "attn_out")
            if need_ctx:
                att_c = _attention(q_c.reshape(bsz, tc, nq), k_c.reshape(bsz, tc, nkv),
                                   jnp.swapaxes(v_c.reshape(bsz, tc, nkv), 1, 2))
                (xc,) = _row_call(_attn_out_kernel, rc, tiles_c, [(att_c.reshape(rc, nq), nq, 0), (xc, d, 0)],
                                  mod_c, [w_out, ng[1]], [(d, F32)], "attn_out_ctx")
            xl = xl_new
        w_in = ffn_w_in[l].astype(BF16)
        w_o = ffn_w_out[l].astype(BF16)
        xl = _ffn(xl, tiles_l, mod_l, ng[2], ng[3], w_in, w_o)
        if need_ctx:
            xc = _ffn(xc, tiles_c, mod_c, ng[2], ng[3], w_in, w_o)
    return xl.reshape(bsz, tl, d)
```
